```python
import numpy as np
import jax, jax.numpy as jnp
from jax import lax

D_MODEL = 2048
BATCH = 2
SEQ = 4096
DEPTH = 4

D_MIX = D_MODEL
GLA_HEADS = 4
GLA_DK = 64
GLA_DV = 128
GLA_GATE_RANK = 16
GLA_GATE_NORMALIZER = 16.0
HGRN_HEADS = 4
HGRN_DK = 128
HGRN_DV = 128
MLA_HEADS = 8
MLA_Q_RANK = 512
MLA_KV_RANK = 512
MLA_NOPE = 128
MLA_ROPE = 64
MLA_DV = 128
ROPE_THETA = 10000.0
D_FF = -(-8 * D_MODEL // (3 * 256)) * 256
CHUNK = 64
Q_BLOCK = 128
EPS = 1e-6

IN_SIZES = (
    GLA_HEADS * GLA_DK,
    GLA_HEADS * GLA_DK,
    GLA_HEADS * GLA_DV,
    GLA_GATE_RANK,
    GLA_HEADS * GLA_DV,
    HGRN_HEADS * HGRN_DK,
    HGRN_HEADS * HGRN_DK,
    HGRN_HEADS * HGRN_DV,
    HGRN_HEADS * HGRN_DV,
    MLA_Q_RANK,
    MLA_KV_RANK,
    MLA_ROPE,
)
D_IN = sum(IN_SIZES)

kernel_name = "hybrid_gla_hgrn2_mla_sandwich_trunk"


def rms_norm(x, gain):
    xf = x.astype(jnp.float32)
    y = xf * lax.rsqrt(jnp.mean(xf * xf, axis=-1, keepdims=True) + EPS)
    return (y * gain.astype(jnp.float32)).astype(x.dtype)


def gated_head_norm(o, gate, gain):
    B, S, H, dv = o.shape
    y = rms_norm(o, gain) * jax.nn.silu(gate.reshape(B, S, H, dv))
    return y.reshape(B, S, H * dv)


def heads(t, n):
    B, S, _ = t.shape
    return t.reshape(B, S, n, -1).transpose(0, 2, 1, 3)


def chunk_gated_linear_attention(q, k, v, log_g, scale):
    B, H, T, dk = q.shape
    dv = v.shape[-1]
    n = T // CHUNK

    def to_chunks(t):
        return t.astype(jnp.float32).reshape(B, H, n, CHUNK, t.shape[-1]).transpose(2, 0, 1, 3, 4)

    qc, kc, vc, gc = to_chunks(q) * scale, to_chunks(k), to_chunks(v), to_chunks(log_g)
    causal = jnp.tril(jnp.ones((CHUNK, CHUNK), dtype=bool))

    def step(S, inp):
        qi, ki, vi, gi = inp
        b = jnp.cumsum(gi, axis=-2)
        diff = b[..., :, None, :] - b[..., None, :, :]
        decay = jnp.exp(jnp.where(causal[:, :, None], diff, -jnp.inf))
        A = jnp.einsum('bhid,bhijd,bhjd->bhij', qi, decay, ki)
        o = A @ vi + jnp.einsum('bhid,bhde->bhie', qi * jnp.exp(b), S)
        b_last = b[..., -1:, :]
        S = jnp.exp(b_last[..., 0, :])[..., None] * S + jnp.einsum(
            'bhjd,bhje->bhde', ki * jnp.exp(b_last - b), vi)
        return S, o

    S0 = jnp.zeros((B, H, dk, dv), jnp.float32)
    _, o = lax.scan(step, S0, (qc, kc, vc, gc))
    return o.transpose(1, 2, 0, 3, 4).reshape(B, H, T, dv)


def apply_rope(x, cos, sin):
    xf = x.astype(jnp.float32)
    x1, x2 = jnp.split(xf, 2, axis=-1)
    y = jnp.concatenate([x1 * cos - x2 * sin, x2 * cos + x1 * sin], axis=-1)
    return y.astype(x.dtype)


def causal_mla_attention(q_nope, q_pe, k_nope, k_pe, v):
    B, H, T, dn = q_nope.shape
    dv = v.shape[-1]
    nb = T // Q_BLOCK
    scale = (MLA_NOPE + MLA_ROPE) ** -0.5
    qn = q_nope.reshape(B, H, nb, Q_BLOCK, dn).transpose(2, 0, 1, 3, 4)
    qp = q_pe.reshape(B, H, nb, Q_BLOCK, MLA_ROPE).transpose(2, 0, 1, 3, 4)
    starts = jnp.arange(nb, dtype=jnp.int32) * Q_BLOCK
    key_idx = jnp.arange(T, dtype=jnp.int32)

    def block(args):
        qn_b, qp_b, start = args
        s = (jnp.einsum('bhqd,bhkd->bhqk', qn_b, k_nope)
             + jnp.einsum('bhqr,bkr->bhqk', qp_b, k_pe)).astype(jnp.float32) * scale
        q_idx = start + jnp.arange(Q_BLOCK, dtype=jnp.int32)
        s = jnp.where(key_idx[None, :] <= q_idx[:, None], s, -jnp.inf)
        p = jax.nn.softmax(s, axis=-1).astype(v.dtype)
        return jnp.einsum('bhqk,bhkd->bhqd', p, v)

    o = lax.map(block, (qn, qp, starts))
    return o.transpose(1, 0, 3, 2, 4).reshape(B, T, H, dv)


def hybrid_mixer(h, cos, sin, lb, w_in, gla_gate_w2, gla_gate_b, gla_out_norm, hgrn_out_norm,
                 mla_q_norm, mla_wq_b, mla_kv_norm, mla_wkv_b, mla_out_norm, w_out):
    B, S, _ = h.shape
    proj = h @ w_in
    split_at = [int(c) for c in np.cumsum(IN_SIZES)[:-1]]
    (gq, gk, gv, g_low, g_out, hq, hf, hi, h_out, qc, kvc, kpe) = jnp.split(proj, split_at, axis=-1)

    log_a = jax.nn.log_sigmoid((g_low @ gla_gate_w2 + gla_gate_b).astype(jnp.float32)) / GLA_GATE_NORMALIZER
    o_gla = chunk_gated_linear_attention(heads(gq, GLA_HEADS), heads(gk, GLA_HEADS), heads(gv, GLA_HEADS),
                                         heads(log_a, GLA_HEADS), GLA_DK ** -0.5)
    y_gla = gated_head_norm(o_gla.transpose(0, 2, 1, 3).astype(h.dtype), g_out, gla_out_norm)

    log_f = jnp.logaddexp(jnp.log(lb), jnp.log1p(-lb) + jax.nn.log_sigmoid(hf.astype(jnp.float32)))
    k_h = 1.0 - jnp.exp(log_f)
    o_hg = chunk_gated_linear_attention(heads(jax.nn.silu(hq), HGRN_HEADS), heads(k_h, HGRN_HEADS),
                                        heads(hi, HGRN_HEADS), heads(log_f, HGRN_HEADS), 1.0)
    y_hg = gated_head_norm(o_hg.transpose(0, 2, 1, 3).astype(h.dtype), h_out, hgrn_out_norm)

    q = (rms_norm(qc, mla_q_norm) @ mla_wq_b).reshape(B, S, MLA_HEADS, MLA_NOPE + MLA_ROPE)
    q_nope, q_pe = q[..., :MLA_NOPE], apply_rope(q[..., MLA_NOPE:], cos[:, :, None, :], sin[:, :, None, :])
    kv = (rms_norm(kvc, mla_kv_norm) @ mla_wkv_b).reshape(B, S, MLA_HEADS, MLA_NOPE + MLA_DV)
    k_nope, v = kv[..., :MLA_NOPE], kv[..., MLA_NOPE:]
    k_pe = apply_rope(kpe, cos, sin)
    o_mla = causal_mla_attention(q_nope.transpose(0, 2, 1, 3), q_pe.transpose(0, 2, 1, 3),
                                 k_nope.transpose(0, 2, 1, 3), k_pe, v.transpose(0, 2, 1, 3))
    y_mla = rms_norm(o_mla.reshape(B, S, MLA_HEADS * MLA_DV), mla_out_norm)

    return jnp.concatenate([y_gla, y_hg, y_mla], axis=-1) @ w_out


def setup_inputs(seed: int = 0) -> dict:
    key = jax.random.key(seed)
    ks = jax.random.split(key, 24)
    L = DEPTH

    def w(k, shape, fan_in):
        return jax.random.normal(k, shape, jnp.float32) * fan_in ** -0.5

    def gain(k, shape):
        return 1.0 + 0.02 * jax.random.normal(k, shape, jnp.float32)

    return {
        "x": jax.random.normal(ks[0], (BATCH, SEQ, D_MODEL), jnp.float32),
        "positions": jnp.broadcast_to(jnp.arange(SEQ, dtype=jnp.int32), (BATCH, SEQ)),
        "attn_pre_norm": gain(ks[1], (L, D_MODEL)),
        "w_in": w(ks[2], (L, D_MODEL, D_IN), D_MODEL),
        "gla_gate_w2": w(ks[3], (L, GLA_GATE_RANK, GLA_HEADS * GLA_DK), GLA_GATE_RANK),
        "gla_gate_b": 0.1 * jax.random.normal(ks[4], (L, GLA_HEADS * GLA_DK), jnp.float32),
        "gla_out_norm": gain(ks[5], (L, GLA_DV)),
        "hgrn_lb_logits": 0.5 * jax.random.normal(ks[6], (L, HGRN_HEADS * HGRN_DK), jnp.float32),
        "hgrn_out_norm": gain(ks[7], (L, HGRN_DV)),
        "mla_q_norm": gain(ks[8], (L, MLA_Q_RANK)),
        "mla_wq_b": w(ks[9], (L, MLA_Q_RANK, MLA_HEADS * (MLA_NOPE + MLA_ROPE)), MLA_Q_RANK),
        "mla_kv_norm": gain(ks[10], (L, MLA_KV_RANK)),
        "mla_wkv_b": w(ks[11], (L, MLA_KV_RANK, MLA_HEADS * (MLA_NOPE + MLA_DV)), MLA_KV_RANK),
        "mla_out_norm": gain(ks[12], (L, MLA_HEADS * MLA_DV)),
        "w_out": w(ks[13], (L, D_MIX, D_MODEL), D_MIX),
        "attn_post_norm": gain(ks[14], (L, D_MODEL)),
        "ffn_pre_norm": gain(ks[15], (L, D_MODEL)),
        "w_gate": w(ks[16], (L, D_MODEL, D_FF), D_MODEL),
        "w_up": w(ks[17], (L, D_MODEL, D_FF), D_MODEL),
        "w_down": w(ks[18], (L, D_FF, D_MODEL), D_FF),
        "ffn_post_norm": gain(ks[19], (L, D_MODEL)),
    }


def reference(x, positions, attn_pre_norm, w_in, gla_gate_w2, gla_gate_b, gla_out_norm,
              hgrn_lb_logits, hgrn_out_norm, mla_q_norm, mla_wq_b, mla_kv_norm, mla_wkv_b,
              mla_out_norm, w_out, attn_post_norm, ffn_pre_norm, w_gate, w_up, w_down, ffn_post_norm):
    inv_freq = ROPE_THETA ** (-jnp.arange(0, MLA_ROPE, 2, dtype=jnp.float32) / MLA_ROPE)
    ang = positions.astype(jnp.float32)[..., None] * inv_freq
    cos, sin = jnp.cos(ang), jnp.sin(ang)
    cs = jnp.cumsum(jax.nn.softmax(hgrn_lb_logits.astype(jnp.float32), axis=0), axis=0)
    lower_bounds = cs - cs[0]

    for l in range(DEPTH):
        h = rms_norm(x, attn_pre_norm[l])
        m = hybrid_mixer(h, cos, sin, lower_bounds[l], w_in[l], gla_gate_w2[l], gla_gate_b[l],
                         gla_out_norm[l], hgrn_out_norm[l], mla_q_norm[l], mla_wq_b[l],
                         mla_kv_norm[l], mla_wkv_b[l], mla_out_norm[l], w_out[l])
        x = x + rms_norm(m, attn_post_norm[l])
        u = rms_norm(x, ffn_pre_norm[l])
        f = (jax.nn.silu(u @ w_gate[l]) * (u @ w_up[l])) @ w_down[l]
        x = x + rms_norm(f, ffn_post_norm[l])
    return x
```

```python
import functools

import numpy as np
import jax
import jax.numpy as jnp
from jax import lax
from jax.experimental import pallas as pl
from jax.experimental.pallas import tpu as pltpu

F32 = jnp.float32
BF16 = jnp.bfloat16

D_MODEL = 2048
BATCH = 2
SEQ = 4096
DEPTH = 4
TOKENS = BATCH * SEQ
GLA_HEADS = 4
GLA_DK = 64
GLA_DV = 128
GLA_GATE_RANK = 16
GLA_GATE_NORMALIZER = 16.0
HGRN_HEADS = 4
HGRN_DK = 128
HGRN_DV = 128
MLA_HEADS = 8
MLA_Q_RANK = 512
MLA_KV_RANK = 512
MLA_NOPE = 128
MLA_ROPE = 64
MLA_DV = 128
ROPE_THETA = 10000.0
D_FF = -(-8 * D_MODEL // (3 * 256)) * 256
EPS = 1e-6

LANE = 128
VMEM_LIMIT = 56 * 1024 * 1024

CB_GQ, CB_GK, CB_GV, CB_GOUT = 0, 4, 8, 12
CB_HQ, CB_HF, CB_HI, CB_HOUT = 16, 20, 24, 28
CB_QC, CB_KVC = 32, 36
CB_KPE, CB_GLOW = 40, 41
PROJ_COLS = 42 * LANE

PROJ_TM, PROJ_TN = 1024, 768
REC_CHUNK = 128
REC_TB = 512
REC_LEVELS = tuple(2 ** p for p in range(1, 8))
MLA_TM = 512
ATT_TQ = 512
OUT_TM = 512
FFN_TM, FFN_TF = 512, 512


def _cparams(*sem):
    return pltpu.CompilerParams(dimension_semantics=sem, vmem_limit_bytes=VMEM_LIMIT)


def _dot(a, b):
    return jnp.dot(a, b, preferred_element_type=F32)


def _dot_nt(a, b):
    return lax.dot_general(a, b, (((1,), (1,)), ((), ())), preferred_element_type=F32)


def _dot_tn(a, b):
    return lax.dot_general(a, b, (((0,), (0,)), ((), ())), preferred_element_type=F32)


def _split_bf16(x):
    hi = x.astype(BF16)
    lo = (x - hi.astype(F32)).astype(BF16)
    return hi, lo


def _rms(x, gain):
    ms = jnp.mean(x * x, axis=-1, keepdims=True)
    return x * lax.rsqrt(ms + EPS) * gain


def _sigmoid(x):
    return 1.0 / (1.0 + jnp.exp(-x))


def _log_sigmoid(x):
    return jnp.minimum(x, 0.0) - jnp.log1p(jnp.exp(-jnp.abs(x)))


def _norm_matmul_kernel(x_ref, g_ref, w_ref, o_ref, h_ref):
    @pl.when(pl.program_id(1) == 0)
    def _():
        h_ref[...] = _rms(x_ref[...], g_ref[...]).astype(BF16)

    o_ref[...] = _dot(h_ref[...], w_ref[...]).astype(o_ref.dtype)


def _norm_matmul(x, gain, w, out_dtype):
    m, d = x.shape
    n = w.shape[1]
    return pl.pallas_call(
        _norm_matmul_kernel,
        grid=(m // PROJ_TM, n // PROJ_TN),
        in_specs=[
            pl.BlockSpec((PROJ_TM, d), lambda i, j: (i, 0)),
            pl.BlockSpec((1, d), lambda i, j: (0, 0)),
            pl.BlockSpec((d, PROJ_TN), lambda i, j: (0, j)),
        ],
        out_specs=pl.BlockSpec((PROJ_TM, PROJ_TN), lambda i, j: (i, j)),
        out_shape=jax.ShapeDtypeStruct((m, n), out_dtype),
        scratch_shapes=[pltpu.VMEM((PROJ_TM, d), BF16)],
        compiler_params=_cparams("parallel", "arbitrary"),
        name="norm_proj",
    )(x, gain, w)


def _decay_sum_matrix():
    c = REC_CHUNK
    t = np.arange(c)
    blocks = [(t[None, :] <= t[:, None])]
    for m in REC_LEVELS:
        half = m // 2
        pos = t % m
        mid = t - pos + half
        upper = (pos >= half)[:, None] & (t[None, :] >= mid[:, None]) & (t[None, :] <= t[:, None])
        lower = (pos < half)[:, None] & (t[None, :] > t[:, None]) & (t[None, :] < mid[:, None])
        blocks.append(upper | lower)
    blocks.append(t[None, :] > t[:, None])
    return np.concatenate(blocks, axis=0).astype(np.float32)


def _rec_chunk(q, k, v, g, st_ref, sums):
    c = REC_CHUNK
    g_hi, g_lo = _split_bf16(g)
    e = jnp.exp(_dot(sums, g_hi) + _dot(sums, g_lo))
    e_cum = e[0:c]
    e_tail = e[(len(REC_LEVELS) + 1) * c:(len(REC_LEVELS) + 2) * c]
    e_last = e_cum[c - 1:c, :]

    row = lax.broadcasted_iota(jnp.int32, q.shape, 0)
    ri = lax.broadcasted_iota(jnp.int32, (c, c), 0)
    ci = lax.broadcasted_iota(jnp.int32, (c, c), 1)
    v16 = v.astype(BF16)

    a = jnp.where(ri == ci, _dot_nt(q.astype(BF16), k.astype(BF16)), 0.0)
    for lvl, m in enumerate(REC_LEVELS):
        e_m = e[(lvl + 1) * c:(lvl + 2) * c]
        in_upper = (row & (m - 1)) >= (m // 2)
        q_m = jnp.where(in_upper, q * e_m, 0.0).astype(BF16)
        k_m = jnp.where(in_upper, 0.0, k * e_m).astype(BF16)
        a_m = _dot_nt(q_m, k_m)
        a = a + (a_m if m == c else jnp.where((ri ^ ci) < m, a_m, 0.0))

    st = st_ref[...]
    o = _dot(a.astype(BF16), v16) + _dot_nt((q * e_cum).astype(BF16), st.astype(BF16))
    st_ref[...] = st * e_last + _dot_tn(v16, (k * e_tail).astype(BF16))
    return o


def _rec_finish(o, og, gain):
    return _rms(o, gain) * (og * _sigmoid(og))


def _gla_kernel(q_ref, k_ref, v_ref, gl_ref, og_ref, w2_ref, b2_ref, gain_ref, sums_ref,
                y_ref, st_ref):
    @pl.when(pl.program_id(2) == 0)
    def _():
        st_ref[...] = jnp.zeros_like(st_ref)

    sums = sums_ref[...]
    w2_hi, w2_lo = _split_bf16(w2_ref[0])

    def body(ck, carry):
        sl = pl.ds(pl.multiple_of(ck * REC_CHUNK, REC_CHUNK), REC_CHUNK)
        gl_hi, gl_lo = _split_bf16(gl_ref[sl, :])
        z = _dot(gl_hi, w2_hi) + _dot(gl_lo, w2_hi) + _dot(gl_hi, w2_lo) + b2_ref[0]
        g = _log_sigmoid(z) * (1.0 / GLA_GATE_NORMALIZER)
        q = q_ref[sl, :] * (GLA_DK ** -0.5)
        o = _rec_chunk(q, k_ref[sl, :], v_ref[sl, :], g, st_ref, sums)
        y_ref[sl, :] = _rec_finish(o, og_ref[sl, :], gain_ref[...]).astype(y_ref.dtype)
        return carry

    lax.fori_loop(0, REC_TB // REC_CHUNK, body, 0)


def _hgrn_kernel(q_ref, f_ref, v_ref, og_ref, llb_ref, l1m_ref, gain_ref, sums_ref,
                 y_ref, st_ref):
    @pl.when(pl.program_id(2) == 0)
    def _():
        st_ref[...] = jnp.zeros_like(st_ref)

    sums = sums_ref[...]

    def body(ck, carry):
        sl = pl.ds(pl.multiple_of(ck * REC_CHUNK, REC_CHUNK), REC_CHUNK)
        hq = q_ref[sl, :]
        q = hq * _sigmoid(hq)
        lo = llb_ref[0]
        hi = l1m_ref[0] + _log_sigmoid(f_ref[sl, :])
        g = jnp.maximum(lo, hi) + jnp.log1p(jnp.exp(-jnp.abs(lo - hi)))
        k = 1.0 - jnp.exp(g)
        o = _rec_chunk(q, k, v_ref[sl, :], g, st_ref, sums)
        y_ref[sl, :] = _rec_finish(o, og_ref[sl, :], gain_ref[...]).astype(y_ref.dtype)
        return carry

    lax.fori_loop(0, REC_TB // REC_CHUNK, body, 0)


def _rec_call(kernel, name, proj, col_blocks, head_params, shared, n_heads):
    nt = SEQ // REC_TB

    def proj_spec(cb, per_head):
        if per_head:
            return pl.BlockSpec((REC_TB, LANE), lambda b, h, t: (b * nt + t, cb + h))
        return pl.BlockSpec((REC_TB, LANE), lambda b, h, t: (b * nt + t, cb))

    in_specs = [proj_spec(cb, ph) for cb, ph in col_blocks]
    args = [proj] * len(col_blocks)
    for p in head_params:
        in_specs.append(pl.BlockSpec((1,) + p.shape[1:], lambda b, h, t: (h, 0, 0)))
        args.append(p)
    for p in shared:
        in_specs.append(pl.BlockSpec(p.shape, lambda b, h, t: (0, 0)))
        args.append(p)
    return pl.pallas_call(
        kernel,
        grid=(BATCH, n_heads, nt),
        in_specs=in_specs,
        out_specs=pl.BlockSpec((REC_TB, LANE), lambda b, h, t: (b * nt + t, h)),
        out_shape=jax.ShapeDtypeStruct((TOKENS, n_heads * LANE), BF16),
        scratch_shapes=[pltpu.VMEM((LANE, LANE), F32)],
        compiler_params=_cparams("parallel", "parallel", "arbitrary"),
        name=name,
    )(*args)


def _mla_proj_kernel(qc_ref, kvc_ref, kpe_ref, cs_ref, qn_ref, kvn_ref, wq_ref, wkv_ref,
                     q_ref, kn_ref, v_ref, kr_ref):
    scale = (MLA_NOPE + MLA_ROPE) ** -0.5
    cs = cs_ref[...]
    qn = _rms(qc_ref[...], qn_ref[...]).astype(BF16)
    q = _dot(qn, wq_ref[...])
    for h in range(MLA_HEADS):
        lo = h * 2 * LANE
        q_ref[:, lo:lo + LANE] = (q[:, lo:lo + LANE] * scale).astype(q_ref.dtype)
        q_ref[:, lo + LANE:lo + 2 * LANE] = (q[:, lo + LANE:lo + 2 * LANE] * (cs * scale)).astype(q_ref.dtype)
    kvn = _rms(kvc_ref[...], kvn_ref[...]).astype(BF16)
    kv = _dot(kvn, wkv_ref[...])
    nk = MLA_HEADS * MLA_NOPE
    kn_ref[...] = kv[:, :nk].astype(kn_ref.dtype)
    v_ref[...] = kv[:, nk:].astype(v_ref.dtype)
    t = kpe_ref[...] * cs
    kr_ref[...] = (t + pltpu.roll(t, LANE // 2, 1)).astype(kr_ref.dtype)


def _mla_proj(proj, cs, qn_gain, kvn_gain, wq, wkv):
    tm = MLA_TM
    wide = 4 * LANE
    nq = MLA_HEADS * 2 * LANE
    nk = MLA_HEADS * MLA_NOPE
    nv = MLA_HEADS * MLA_DV
    return pl.pallas_call(
        _mla_proj_kernel,
        grid=(TOKENS // tm,),
        in_specs=[
            pl.BlockSpec((tm, wide), lambda i: (i, CB_QC // 4)),
            pl.BlockSpec((tm, wide), lambda i: (i, CB_KVC // 4)),
            pl.BlockSpec((tm, LANE), lambda i: (i, CB_KPE)),
            pl.BlockSpec((tm, LANE), lambda i: (i, 0)),
            pl.BlockSpec((1, MLA_Q_RANK), lambda i: (0, 0)),
            pl.BlockSpec((1, MLA_KV_RANK), lambda i: (0, 0)),
            pl.BlockSpec(wq.shape, lambda i: (0, 0)),
            pl.BlockSpec(wkv.shape, lambda i: (0, 0)),
        ],
        out_specs=[
            pl.BlockSpec((tm, nq), lambda i: (i, 0)),
            pl.BlockSpec((tm, nk), lambda i: (i, 0)),
            pl.BlockSpec((tm, nv), lambda i: (i, 0)),
            pl.BlockSpec((tm, LANE), lambda i: (i, 0)),
        ],
        out_shape=[
            jax.ShapeDtypeStruct((TOKENS, nq), BF16),
            jax.ShapeDtypeStruct((TOKENS, nk), BF16),
            jax.ShapeDtypeStruct((TOKENS, nv), BF16),
            jax.ShapeDtypeStruct((TOKENS, LANE), BF16),
        ],
        compiler_params=_cparams("parallel"),
        name="mla_proj",
    )(proj, proj, proj, cs, qn_gain, kvn_gain, wq, wkv)


def _attn_kernel(q_ref, kn_ref, kr_ref, v_ref, o_ref):
    tq = ATT_TQ
    qi = pl.program_id(2)
    q = q_ref[...]

    def step(j, carry, masked):
        m, l, acc = carry
        sl = pl.ds(pl.multiple_of(j * tq, tq), tq)
        k = jnp.concatenate([kn_ref[sl, :], kr_ref[sl, :]], axis=1)
        s = _dot_nt(q, k)
        if masked:
            ri = lax.broadcasted_iota(jnp.int32, s.shape, 0)
            ci = lax.broadcasted_iota(jnp.int32, s.shape, 1)
            s = jnp.where(ci <= ri, s, -jnp.inf)
        m_new = jnp.maximum(m, jnp.max(s, axis=-1, keepdims=True))
        alpha = jnp.exp(m - m_new)
        p = jnp.exp(s - m_new)
        l = alpha * l + jnp.sum(p, axis=-1, keepdims=True)
        acc = alpha * acc + _dot(p.astype(BF16), v_ref[sl, :])
        return m_new, l, acc

    init = (jnp.full((tq, 1), -jnp.inf, F32), jnp.zeros((tq, 1), F32), jnp.zeros((tq, MLA_DV), F32))
    carry = lax.fori_loop(0, qi, functools.partial(step, masked=False), init)
    _, l, acc = step(qi, carry, masked=True)
    o_ref[...] = (acc / l).astype(o_ref.dtype)


def _attention(q, kn, kr, v):
    tq = ATT_TQ
    nq = SEQ // tq
    return pl.pallas_call(
        _attn_kernel,
        grid=(BATCH, MLA_HEADS, nq),
        in_specs=[
            pl.BlockSpec((tq, 2 * LANE), lambda b, h, i: (b * nq + i, h)),
            pl.BlockSpec((SEQ, LANE), lambda b, h, i: (b, h)),
            pl.BlockSpec((SEQ, LANE), lambda b, h, i: (b, 0)),
            pl.BlockSpec((SEQ, LANE), lambda b, h, i: (b, h)),
        ],
        out_specs=pl.BlockSpec((tq, LANE), lambda b, h, i: (b * nq + i, h)),
        out_shape=jax.ShapeDtypeStruct((TOKENS, MLA_HEADS * MLA_DV), F32),
        compiler_params=_cparams("parallel", "parallel", "arbitrary"),
        name="mla_attention",
    )(q, kn, kr, v)


def _out_proj_kernel(yg_ref, yh_ref, om_ref, x_ref, mg_ref, w_ref, pg_ref, o_ref):
    ym = _rms(om_ref[...], mg_ref[...]).astype(BF16)
    mix = jnp.concatenate([yg_ref[...], yh_ref[...], ym], axis=1)
    m = _dot(mix, w_ref[...])
    o_ref[...] = x_ref[...] + _rms(m, pg_ref[...])


def _out_proj(y_gla, y_hg, o_mla, x, mla_gain, w_out, post_gain):
    tm = OUT_TM
    d = D_MODEL
    return pl.pallas_call(
        _out_proj_kernel,
        grid=(TOKENS // tm,),
        in_specs=[
            pl.BlockSpec((tm, y_gla.shape[1]), lambda i: (i, 0)),
            pl.BlockSpec((tm, y_hg.shape[1]), lambda i: (i, 0)),
            pl.BlockSpec((tm, o_mla.shape[1]), lambda i: (i, 0)),
            pl.BlockSpec((tm, d), lambda i: (i, 0)),
            pl.BlockSpec((1, o_mla.shape[1]), lambda i: (0, 0)),
            pl.BlockSpec(w_out.shape, lambda i: (0, 0)),
            pl.BlockSpec((1, d), lambda i: (0, 0)),
        ],
        out_specs=pl.BlockSpec((tm, d), lambda i: (i, 0)),
        out_shape=jax.ShapeDtypeStruct((TOKENS, d), F32),
        compiler_params=_cparams("parallel"),
        name="out_proj",
    )(y_gla, y_hg, o_mla, x, mla_gain, w_out, post_gain)


def _ffn_kernel(x_ref, ug_ref, wg_ref, wu_ref, wd_ref, pg_ref, o_ref, u_ref, acc_ref):
    j = pl.program_id(1)

    @pl.when(j == 0)
    def _():
        u_ref[...] = _rms(x_ref[...], ug_ref[...]).astype(BF16)
        acc_ref[...] = jnp.zeros_like(acc_ref)

    u = u_ref[...]
    gate = _dot(u, wg_ref[...])
    up = _dot(u, wu_ref[...])
    hid = (gate * _sigmoid(gate) * up).astype(BF16)
    acc_ref[...] += _dot(hid, wd_ref[...])

    @pl.when(j == pl.num_programs(1) - 1)
    def _():
        o_ref[...] = x_ref[...] + _rms(acc_ref[...], pg_ref[...])


def _ffn(x, pre_gain, wg, wu, wd, post_gain):
    tm, tf = FFN_TM, FFN_TF
    d = D_MODEL
    return pl.pallas_call(
        _ffn_kernel,
        grid=(TOKENS // tm, D_FF // tf),
        in_specs=[
            pl.BlockSpec((tm, d), lambda i, j: (i, 0)),
            pl.BlockSpec((1, d), lambda i, j: (0, 0)),
            pl.BlockSpec((d, tf), lambda i, j: (0, j)),
            pl.BlockSpec((d, tf), lambda i, j: (0, j)),
            pl.BlockSpec((tf, d), lambda i, j: (j, 0)),
            pl.BlockSpec((1, d), lambda i, j: (0, 0)),
        ],
        out_specs=pl.BlockSpec((tm, d), lambda i, j: (i, 0)),
        out_shape=jax.ShapeDtypeStruct((TOKENS, d), F32),
        scratch_shapes=[pltpu.VMEM((tm, d), BF16), pltpu.VMEM((tm, d), F32)],
        compiler_params=_cparams("parallel", "arbitrary"),
        name="ffn",
    )(x, pre_gain, wg, wu, wd, post_gain)


def _rot_cols(w):
    half = MLA_ROPE // 2
    return jnp.concatenate([-w[..., half:], w[..., :half]], axis=-1)


def _pad_heads(w, heads, width):
    lead = w.shape[:-1]
    w = w.reshape(lead + (heads, width))
    w = jnp.pad(w, [(0, 0)] * len(lead) + [(0, 0), (0, LANE - width)])
    return w.reshape(lead + (heads * LANE,))


def _layout_w_in(w_in):
    sizes = (GLA_HEADS * GLA_DK, GLA_HEADS * GLA_DK, GLA_HEADS * GLA_DV, GLA_GATE_RANK,
             GLA_HEADS * GLA_DV, HGRN_HEADS * HGRN_DK, HGRN_HEADS * HGRN_DK, HGRN_HEADS * HGRN_DV,
             HGRN_HEADS * HGRN_DV, MLA_Q_RANK, MLA_KV_RANK, MLA_ROPE)
    split_at = [int(c) for c in np.cumsum(sizes)[:-1]]
    gq, gk, gv, g_low, g_out, hq, hf, hi, h_out, qc, kvc, kpe = jnp.split(w_in, split_at, axis=-1)
    g_low = jnp.pad(g_low, [(0, 0)] * (g_low.ndim - 1) + [(0, LANE - GLA_GATE_RANK)])
    cols = [_pad_heads(gq, GLA_HEADS, GLA_DK), _pad_heads(gk, GLA_HEADS, GLA_DK), gv, g_out,
            hq, hf, hi, h_out, qc, kvc, kpe, _rot_cols(kpe), g_low]
    out = jnp.concatenate(cols, axis=-1).astype(BF16)
    assert out.shape[-1] == PROJ_COLS
    return out


def _layout_wq(wq_b):
    lead = wq_b.shape[:-1]
    w = wq_b.reshape(lead + (MLA_HEADS, MLA_NOPE + MLA_ROPE))
    nope, pe = w[..., :MLA_NOPE], w[..., MLA_NOPE:]
    w = jnp.concatenate([nope, pe, _rot_cols(pe)], axis=-1)
    return w.reshape(lead + (MLA_HEADS * 2 * LANE,)).astype(BF16)


def _layout_wkv(wkv_b):
    lead = wkv_b.shape[:-1]
    w = wkv_b.reshape(lead + (MLA_HEADS, MLA_NOPE + MLA_DV))
    k = w[..., :MLA_NOPE].reshape(lead + (MLA_HEADS * MLA_NOPE,))
    v = w[..., MLA_NOPE:].reshape(lead + (MLA_HEADS * MLA_DV,))
    return jnp.concatenate([k, v], axis=-1).astype(BF16)


def kernel(x, positions, attn_pre_norm, w_in, gla_gate_w2, gla_gate_b, gla_out_norm, hgrn_lb_logits,
           hgrn_out_norm, mla_q_norm, mla_wq_b, mla_kv_norm, mla_wkv_b, mla_out_norm, w_out,
           attn_post_norm, ffn_pre_norm, w_gate, w_up, w_down, ffn_post_norm):
    depth = w_in.shape[0]
    inv_freq = ROPE_THETA ** (-jnp.arange(0, MLA_ROPE, 2, dtype=F32) / MLA_ROPE)
    ang = positions.astype(F32).reshape(TOKENS, 1) * inv_freq
    cs = jnp.concatenate([jnp.cos(ang), jnp.cos(ang), jnp.sin(ang), jnp.sin(ang)], axis=-1)
    csum = jnp.cumsum(jax.nn.softmax(hgrn_lb_logits.astype(F32), axis=0), axis=0)
    lb = csum - csum[0]
    log_lb = jnp.log(lb).reshape(depth, HGRN_HEADS, 1, HGRN_DK)
    log_1m_lb = jnp.log1p(-lb).reshape(depth, HGRN_HEADS, 1, HGRN_DK)

    w_in_l = _layout_w_in(w_in)
    wq_l = _layout_wq(mla_wq_b)
    wkv_l = _layout_wkv(mla_wkv_b)
    w_out_l = w_out.astype(BF16)
    w_gate_l, w_up_l, w_down_l = w_gate.astype(BF16), w_up.astype(BF16), w_down.astype(BF16)
    w2 = gla_gate_w2.reshape(depth, GLA_GATE_RANK, GLA_HEADS, GLA_DK).transpose(0, 2, 1, 3)
    w2 = jnp.pad(w2, [(0, 0), (0, 0), (0, LANE - GLA_GATE_RANK), (0, LANE - GLA_DK)])
    b2 = jnp.pad(gla_gate_b.reshape(depth, GLA_HEADS, 1, GLA_DK), [(0, 0), (0, 0), (0, 0), (0, LANE - GLA_DK)])
    sums = jnp.asarray(_decay_sum_matrix(), dtype=BF16)

    def row(p, l):
        return p[l].reshape(1, -1)

    xf = x.reshape(TOKENS, D_MODEL)
    for l in range(depth):
        proj = _norm_matmul(xf, row(attn_pre_norm, l), w_in_l[l], F32)
        y_gla = _rec_call(
            _gla_kernel, "gla_recurrence", proj,
            [(CB_GQ, True), (CB_GK, True), (CB_GV, True), (CB_GLOW, False), (CB_GOUT, True)],
            [w2[l], b2[l]], [row(gla_out_norm, l), sums], GLA_HEADS)
        y_hg = _rec_call(
            _hgrn_kernel, "hgrn_recurrence", proj,
            [(CB_HQ, True), (CB_HF, True), (CB_HI, True), (CB_HOUT, True)],
            [log_lb[l], log_1m_lb[l]], [row(hgrn_out_norm, l), sums], HGRN_HEADS)
        q, kn, v, kr = _mla_proj(proj, cs, row(mla_q_norm, l), row(mla_kv_norm, l), wq_l[l], wkv_l[l])
        o_mla = _attention(q, kn, kr, v)
        xf = _out_proj(y_gla, y_hg, o_mla, xf, row(mla_out_norm, l), w_out_l[l], row(attn_post_norm, l))
        xf = _ffn(xf, row(ffn_pre_norm, l), w_gate_l[l], w_up_l[l], w_down_l[l], row(ffn_post_norm, l))
    return xf.reshape(x.shape)
```

```python
import functools

import numpy as np
import jax
import jax.numpy as jnp
from jax import lax
from jax.experimental import pallas as pl
from jax.experimental.pallas import tpu as pltpu

F32 = jnp.float32
BF16 = jnp.bfloat16

D_MODEL = 2048
BATCH = 2
SEQ = 4096
DEPTH = 4
TOKENS = BATCH * SEQ
GLA_HEADS = 4
GLA_DK = 64
GLA_DV = 128
GLA_GATE_RANK = 16
GLA_GATE_NORMALIZER = 16.0
HGRN_HEADS = 4
HGRN_DK = 128
HGRN_DV = 128
MLA_HEADS = 8
MLA_Q_RANK = 512
MLA_KV_RANK = 512
MLA_NOPE = 128
MLA_ROPE = 64
MLA_DV = 128
ROPE_THETA = 10000.0
D_FF = -(-8 * D_MODEL // (3 * 256)) * 256
EPS = 1e-6
LOG2_E = float(np.log2(np.e))

LANE = 128
VMEM_LIMIT = 56 * 1024 * 1024

CB_GQ, CB_GK, CB_GV, CB_GOUT = 0, 4, 8, 12
CB_HQ, CB_HF, CB_HI, CB_HOUT = 16, 20, 24, 28
CB_QC, CB_KVC = 32, 36
CB_KPE, CB_GLOW = 40, 41
PROJ_COLS = 42 * LANE

PROJ_TM, PROJ_TN = 1024, 768
REC_HEADS = 4
REC_CHUNK = 128
REC_TB = 256
REC_LEVELS = tuple(2 ** p for p in range(1, 8))
MLA_TM = 512
ATT_TQ = 512
ATT_HEADS = 4
VT_ROWS = MLA_DV + 16
OUT_TM = 512
FFN_TM, FFN_TF = 512, 512


def _cparams(*sem):
    return pltpu.CompilerParams(dimension_semantics=sem, vmem_limit_bytes=VMEM_LIMIT)


def _dot(a, b):
    return jnp.dot(a, b, preferred_element_type=F32)


def _dot_nt(a, b):
    return lax.dot_general(a, b, (((1,), (1,)), ((), ())), preferred_element_type=F32)


def _dot_tn(a, b):
    return lax.dot_general(a, b, (((0,), (0,)), ((), ())), preferred_element_type=F32)


def _split_bf16(x):
    hi = x.astype(BF16)
    lo = (x - hi.astype(F32)).astype(BF16)
    return hi, lo


def _rms(x, gain):
    ms = jnp.mean(x * x, axis=-1, keepdims=True)
    return x * lax.rsqrt(ms + EPS) * gain


def _sigmoid(x):
    return 1.0 / (1.0 + jnp.exp(-x))


def _softplus_neg_abs(x):
    return jnp.log(1.0 + jnp.exp(-jnp.abs(x)))


def _log_sigmoid(x):
    return jnp.minimum(x, 0.0) - _softplus_neg_abs(x)


def _layer_spec(block, index_map, layer):
    return pl.BlockSpec((None,) + block, lambda *g: (layer,) + index_map(*g))


def _norm_matmul_kernel(x_ref, g_ref, w_ref, o_ref, h_ref):
    @pl.when(pl.program_id(1) == 0)
    def _():
        h_ref[...] = _rms(x_ref[...], g_ref[...]).astype(BF16)

    o_ref[...] = _dot(h_ref[...], w_ref[...]).astype(o_ref.dtype)


def _norm_matmul(x, gain, w, layer, out_dtype):
    m, d = x.shape
    n = w.shape[-1]
    return pl.pallas_call(
        _norm_matmul_kernel,
        grid=(m // PROJ_TM, n // PROJ_TN),
        in_specs=[
            pl.BlockSpec((PROJ_TM, d), lambda i, j: (i, 0)),
            pl.BlockSpec((1, d), lambda i, j: (0, 0)),
            _layer_spec((d, PROJ_TN), lambda i, j: (0, j), layer),
        ],
        out_specs=pl.BlockSpec((PROJ_TM, PROJ_TN), lambda i, j: (i, j)),
        out_shape=jax.ShapeDtypeStruct((m, n), out_dtype),
        scratch_shapes=[pltpu.VMEM((PROJ_TM, d), BF16)],
        compiler_params=_cparams("parallel", "arbitrary"),
        name="norm_proj",
    )(x, gain, w)


def _decay_sum_matrix():
    c = REC_CHUNK
    t = np.arange(c)
    blocks = [(t[None, :] <= t[:, None])]
    for m in REC_LEVELS:
        half = m // 2
        pos = t % m
        mid = t - pos + half
        upper = (pos >= half)[:, None] & (t[None, :] >= mid[:, None]) & (t[None, :] <= t[:, None])
        lower = (pos < half)[:, None] & (t[None, :] > t[:, None]) & (t[None, :] < mid[:, None])
        blocks.append(upper | lower)
    blocks.append(t[None, :] > t[:, None])
    sums = np.concatenate(blocks, axis=0).astype(np.float32)
    return np.concatenate([sums, sums], axis=1)


def _pair_level_codes():
    c = REC_CHUNK
    ri = lax.broadcasted_iota(jnp.int32, (c, c), 0)
    ci = lax.broadcasted_iota(jnp.int32, (c, c), 1)
    lvl = 32 - lax.clz(ri ^ ci)
    code = jnp.where(ri > ci, lvl, jnp.where(ri == ci, 0, -1))
    return code.astype(F32).astype(BF16)


def _rec_finish(o, og, gain):
    return _rms(o, gain) * (og * _sigmoid(og))


def _rec_kernel(kind, *refs):
    if kind == "gla":
        q_ref, k_ref, v_ref, gl_ref, og_ref, w2_ref, b2_ref, gain_ref, sums_ref, y_ref, st_ref = refs
    else:
        q_ref, f_ref, v_ref, og_ref, llb_ref, l1m_ref, gain_ref, sums_ref, y_ref, st_ref = refs

    @pl.when(pl.program_id(1) == 0)
    def _():
        st_ref[...] = jnp.zeros_like(st_ref)

    c = REC_CHUNK
    nl = len(REC_LEVELS)
    sums = sums_ref[...]
    gain = gain_ref[...]
    code = _pair_level_codes()
    chunks = [slice(ck * c, (ck + 1) * c) for ck in range(REC_TB // c)]
    lanes = [slice(h * LANE, (h + 1) * LANE) for h in range(REC_HEADS)]
    pairs = [(ck, h) for ck in range(len(chunks)) for h in range(REC_HEADS)]
    if kind == "gla":
        w2_hi, w2_lo = _split_bf16(w2_ref[...])

    qs, ks, es = [], [], []
    for sl in chunks:
        if kind == "gla":
            gl_hi, gl_lo = _split_bf16(gl_ref[sl, :])
            z = _dot(gl_hi, w2_hi) + _dot(gl_lo, w2_hi) + _dot(gl_hi, w2_lo) + b2_ref[...]
            g = _log_sigmoid(z) * (1.0 / GLA_GATE_NORMALIZER)
            q = q_ref[sl, :] * (GLA_DK ** -0.5)
            k = k_ref[sl, :]
        else:
            hq = q_ref[sl, :]
            q = hq * _sigmoid(hq)
            lo = llb_ref[...]
            hi = l1m_ref[...] + _log_sigmoid(f_ref[sl, :])
            g = jnp.maximum(lo, hi) + _softplus_neg_abs(lo - hi)
            k = 1.0 - jnp.exp(g)
        g_hi, g_lo = _split_bf16(g * LOG2_E)
        es.append(jnp.exp2(_dot(sums, jnp.concatenate([g_hi, g_lo], axis=0))))
        qs.append(q)
        ks.append(k)

    q16 = [q.astype(BF16) for q in qs]
    k16 = [k.astype(BF16) for k in ks]
    raw = {(ck, h): _dot_nt(q16[ck][:, lanes[h]], k16[ck][:, lanes[h]]) for ck, h in pairs}
    a16 = {p: jnp.where(code == 0, raw[p].astype(BF16), jnp.zeros((c, c), BF16)) for p in pairs}
    for lvl in range(1, nl + 1):
        e16 = [e[lvl * c:(lvl + 1) * c].astype(BF16) for e in es]
        qm = [q16[ck] * e16[ck] for ck in range(len(chunks))]
        km = [k16[ck] * e16[ck] for ck in range(len(chunks))]
        raw = {(ck, h): _dot_nt(qm[ck][:, lanes[h]], km[ck][:, lanes[h]]) for ck, h in pairs}
        a16 = {p: jnp.where(code == lvl, raw[p].astype(BF16), a16[p]) for p in pairs}

    st = [st_ref[h] for h in range(REC_HEADS)]
    for ck, sl in enumerate(chunks):
        e_cum = es[ck][0:c]
        e_tail = es[ck][(nl + 1) * c:(nl + 2) * c]
        q_in = (qs[ck] * e_cum).astype(BF16)
        k_out = (ks[ck] * e_tail).astype(BF16)
        v16 = v_ref[sl, :].astype(BF16)
        for h in range(REC_HEADS):
            ls = lanes[h]
            o = _dot(a16[ck, h], v16[:, ls]) + _dot_nt(q_in[:, ls], st[h].astype(BF16))
            st[h] = st[h] * e_cum[c - 1:c, ls] + _dot_tn(v16[:, ls], k_out[:, ls])
            y_ref[sl, ls] = _rec_finish(o, og_ref[sl, ls], gain).astype(y_ref.dtype)
    for h in range(REC_HEADS):
        st_ref[h] = st[h]


def _rec_call(kind, proj, col_blocks, params):
    nt = SEQ // REC_TB
    in_specs, args = [], []
    for cb, width in col_blocks:
        in_specs.append(pl.BlockSpec((REC_TB, width * LANE),
                                     lambda b, t, cb=cb, width=width: (b * nt + t, cb // width)))
        args.append(proj)
    for p in params:
        in_specs.append(pl.BlockSpec(p.shape, lambda b, t: (0, 0)))
        args.append(p)
    wide = REC_HEADS * LANE
    return pl.pallas_call(
        functools.partial(_rec_kernel, kind),
        grid=(BATCH, nt),
        in_specs=in_specs,
        out_specs=pl.BlockSpec((REC_TB, wide), lambda b, t: (b * nt + t, 0)),
        out_shape=jax.ShapeDtypeStruct((TOKENS, wide), BF16),
        scratch_shapes=[pltpu.VMEM((REC_HEADS, LANE, LANE), F32)],
        compiler_params=_cparams("parallel", "arbitrary"),
        name=kind + "_recurrence",
    )(*args)


def _mla_proj_kernel(qc_ref, kvc_ref, kpe_ref, cs_ref, qn_ref, kvn_ref, wq_ref, wk_ref, wvt_ref,
                     q_ref, kn_ref, vt_ref, kr_ref):
    scale = (MLA_NOPE + MLA_ROPE) ** -0.5
    cs = cs_ref[...]
    qn = _rms(qc_ref[...], qn_ref[...]).astype(BF16)
    q = _dot(qn, wq_ref[...])
    for h in range(MLA_HEADS):
        lo = h * 2 * LANE
        q_ref[:, lo:lo + LANE] = (q[:, lo:lo + LANE] * scale).astype(q_ref.dtype)
        q_ref[:, lo + LANE:lo + 2 * LANE] = (q[:, lo + LANE:lo + 2 * LANE] * (cs * scale)).astype(q_ref.dtype)
    kvn = _rms(kvc_ref[...], kvn_ref[...]).astype(BF16)
    kn_ref[...] = _dot(kvn, wk_ref[...]).astype(kn_ref.dtype)
    vt = _dot_nt(wvt_ref[...], kvn)
    ones = jnp.ones((VT_ROWS - MLA_DV, vt.shape[1]), vt_ref.dtype)
    for h in range(MLA_HEADS):
        vt_ref[h * VT_ROWS:h * VT_ROWS + MLA_DV, :] = vt[h * MLA_DV:(h + 1) * MLA_DV].astype(vt_ref.dtype)
        vt_ref[h * VT_ROWS + MLA_DV:(h + 1) * VT_ROWS, :] = ones
    t = kpe_ref[...] * cs
    kr_ref[...] = (t + pltpu.roll(t, LANE // 2, 1)).astype(kr_ref.dtype)


def _mla_proj(proj, cs, qn_gain, kvn_gain, wq, wk, wvt, layer):
    tm = MLA_TM
    wide = 4 * LANE
    nq = MLA_HEADS * 2 * LANE
    nk = MLA_HEADS * MLA_NOPE
    nvt = MLA_HEADS * VT_ROWS
    return pl.pallas_call(
        _mla_proj_kernel,
        grid=(TOKENS // tm,),
        in_specs=[
            pl.BlockSpec((tm, wide), lambda i: (i, CB_QC // 4)),
            pl.BlockSpec((tm, wide), lambda i: (i, CB_KVC // 4)),
            pl.BlockSpec((tm, LANE), lambda i: (i, CB_KPE)),
            pl.BlockSpec((tm, LANE), lambda i: (i, 0)),
            pl.BlockSpec((1, MLA_Q_RANK), lambda i: (0, 0)),
            pl.BlockSpec((1, MLA_KV_RANK), lambda i: (0, 0)),
            _layer_spec(wq.shape[1:], lambda i: (0, 0), layer),
            _layer_spec(wk.shape[1:], lambda i: (0, 0), layer),
            _layer_spec(wvt.shape[1:], lambda i: (0, 0), layer),
        ],
        out_specs=[
            pl.BlockSpec((tm, nq), lambda i: (i, 0)),
            pl.BlockSpec((tm, nk), lambda i: (i, 0)),
            pl.BlockSpec((nvt, tm), lambda i: (0, i)),
            pl.BlockSpec((tm, LANE), lambda i: (i, 0)),
        ],
        out_shape=[
            jax.ShapeDtypeStruct((TOKENS, nq), BF16),
            jax.ShapeDtypeStruct((TOKENS, nk), BF16),
            jax.ShapeDtypeStruct((nvt, TOKENS), BF16),
            jax.ShapeDtypeStruct((TOKENS, LANE), BF16),
        ],
        compiler_params=_cparams("parallel"),
        name="mla_proj",
    )(proj, proj, proj, cs, qn_gain, kvn_gain, wq, wk, wvt)


def _attn_kernel(q_ref, kn_ref, kr_ref, vt_ref, o_ref):
    tq = ATT_TQ
    qi = pl.program_id(2)
    qs = [q_ref[:, h * 2 * LANE:(h + 1) * 2 * LANE] for h in range(ATT_HEADS)]

    def step(j, state, masked):
        start = pl.multiple_of(j * tq, tq)
        kr = kr_ref[pl.ds(start, tq), :]
        s_blocks = []
        for h in range(ATT_HEADS):
            k = jnp.concatenate([kn_ref[pl.ds(start, tq), h * LANE:(h + 1) * LANE], kr], axis=1)
            s_blocks.append(_dot_nt(k, qs[h]))
        out = []
        for h in range(ATT_HEADS):
            m, acc = state[h]
            s = s_blocks[h]
            if masked:
                ki = lax.broadcasted_iota(jnp.int32, s.shape, 0)
                qj = lax.broadcasted_iota(jnp.int32, s.shape, 1)
                s = jnp.where(ki <= qj, s, -jnp.inf)
            m_new = jnp.maximum(m, jnp.max(s, axis=0, keepdims=True))
            alpha = jnp.exp(m - m_new)
            p = jnp.exp((s - m_new).astype(BF16))
            acc = alpha * acc + _dot(vt_ref[h * VT_ROWS:(h + 1) * VT_ROWS, pl.ds(start, tq)], p)
            out.append((m_new, acc))
        return tuple(out)

    init = tuple((jnp.full((1, tq), -jnp.inf, F32), jnp.zeros((VT_ROWS, tq), F32)) for _ in range(ATT_HEADS))
    state = lax.fori_loop(0, qi, functools.partial(step, masked=False), init)
    final = step(qi, state, masked=True)
    for h in range(ATT_HEADS):
        _, acc = final[h]
        o = acc[:MLA_DV] / acc[MLA_DV:MLA_DV + 1]
        o_ref[:, h * LANE:(h + 1) * LANE] = o.T.astype(o_ref.dtype)


def _attention(q, kn, kr, vt):
    tq = ATT_TQ
    nq = SEQ // tq
    hw = ATT_HEADS * LANE
    return pl.pallas_call(
        _attn_kernel,
        grid=(BATCH, MLA_HEADS // ATT_HEADS, nq),
        in_specs=[
            pl.BlockSpec((tq, 2 * hw), lambda b, h, i: (b * nq + i, h)),
            pl.BlockSpec((SEQ, hw), lambda b, h, i: (b, h)),
            pl.BlockSpec((SEQ, LANE), lambda b, h, i: (b, 0)),
            pl.BlockSpec((ATT_HEADS * VT_ROWS, SEQ), lambda b, h, i: (h, b)),
        ],
        out_specs=pl.BlockSpec((tq, hw), lambda b, h, i: (b * nq + i, h)),
        out_shape=jax.ShapeDtypeStruct((TOKENS, MLA_HEADS * MLA_DV), F32),
        compiler_params=_cparams("parallel", "parallel", "arbitrary"),
        name="mla_attention",
    )(q, kn, kr, vt)


def _out_proj_kernel(yg_ref, yh_ref, om_ref, x_ref, mg_ref, w_ref, pg_ref, o_ref):
    ym = _rms(om_ref[...], mg_ref[...]).astype(BF16)
    mix = jnp.concatenate([yg_ref[...], yh_ref[...], ym], axis=1)
    m = _dot(mix, w_ref[...])
    o_ref[...] = x_ref[...] + _rms(m, pg_ref[...])


def _out_proj(y_gla, y_hg, o_mla, x, mla_gain, w_out, layer, post_gain):
    tm = OUT_TM
    d = D_MODEL
    return pl.pallas_call(
        _out_proj_kernel,
        grid=(TOKENS // tm,),
        in_specs=[
            pl.BlockSpec((tm, y_gla.shape[1]), lambda i: (i, 0)),
            pl.BlockSpec((tm, y_hg.shape[1]), lambda i: (i, 0)),
            pl.BlockSpec((tm, o_mla.shape[1]), lambda i: (i, 0)),
            pl.BlockSpec((tm, d), lambda i: (i, 0)),
            pl.BlockSpec((1, o_mla.shape[1]), lambda i: (0, 0)),
            _layer_spec(w_out.shape[1:], lambda i: (0, 0), layer),
            pl.BlockSpec((1, d), lambda i: (0, 0)),
        ],
        out_specs=pl.BlockSpec((tm, d), lambda i: (i, 0)),
        out_shape=jax.ShapeDtypeStruct((TOKENS, d), F32),
        compiler_params=_cparams("parallel"),
        name="out_proj",
    )(y_gla, y_hg, o_mla, x, mla_gain, w_out, post_gain)


def _ffn_kernel(x_ref, ug_ref, wg_ref, wu_ref, wd_ref, pg_ref, o_ref, u_ref, acc_ref):
    j = pl.program_id(1)

    @pl.when(j == 0)
    def _():
        u_ref[...] = _rms(x_ref[...], ug_ref[...]).astype(BF16)
        acc_ref[...] = jnp.zeros_like(acc_ref)

    u = u_ref[...]
    gate = _dot(u, wg_ref[...])
    up = _dot(u, wu_ref[...])
    hid = (gate * _sigmoid(gate) * up).astype(BF16)
    acc_ref[...] += _dot(hid, wd_ref[...])

    @pl.when(j == pl.num_programs(1) - 1)
    def _():
        o_ref[...] = x_ref[...] + _rms(acc_ref[...], pg_ref[...])


def _ffn(x, pre_gain, wg, wu, wd, layer, post_gain):
    tm, tf = FFN_TM, FFN_TF
    d = D_MODEL
    return pl.pallas_call(
        _ffn_kernel,
        grid=(TOKENS // tm, D_FF // tf),
        in_specs=[
            pl.BlockSpec((tm, d), lambda i, j: (i, 0)),
            pl.BlockSpec((1, d), lambda i, j: (0, 0)),
            _layer_spec((d, tf), lambda i, j: (0, j), layer),
            _layer_spec((d, tf), lambda i, j: (0, j), layer),
            _layer_spec((tf, d), lambda i, j: (j, 0), layer),
            pl.BlockSpec((1, d), lambda i, j: (0, 0)),
        ],
        out_specs=pl.BlockSpec((tm, d), lambda i, j: (i, 0)),
        out_shape=jax.ShapeDtypeStruct((TOKENS, d), F32),
        scratch_shapes=[pltpu.VMEM((tm, d), BF16), pltpu.VMEM((tm, d), F32)],
        compiler_params=_cparams("parallel", "arbitrary"),
        name="ffn",
    )(x, pre_gain, wg, wu, wd, post_gain)


def _rot_cols(w):
    half = MLA_ROPE // 2
    return jnp.concatenate([-w[..., half:], w[..., :half]], axis=-1)


def _pad_heads(w, heads, width):
    lead = w.shape[:-1]
    w = w.reshape(lead + (heads, width))
    w = jnp.pad(w, [(0, 0)] * len(lead) + [(0, 0), (0, LANE - width)])
    return w.reshape(lead + (heads * LANE,))


def _layout_w_in(w_in):
    sizes = (GLA_HEADS * GLA_DK, GLA_HEADS * GLA_DK, GLA_HEADS * GLA_DV, GLA_GATE_RANK,
             GLA_HEADS * GLA_DV, HGRN_HEADS * HGRN_DK, HGRN_HEADS * HGRN_DK, HGRN_HEADS * HGRN_DV,
             HGRN_HEADS * HGRN_DV, MLA_Q_RANK, MLA_KV_RANK, MLA_ROPE)
    split_at = [int(c) for c in np.cumsum(sizes)[:-1]]
    gq, gk, gv, g_low, g_out, hq, hf, hi, h_out, qc, kvc, kpe = jnp.split(w_in, split_at, axis=-1)
    g_low = jnp.pad(g_low, [(0, 0)] * (g_low.ndim - 1) + [(0, LANE - GLA_GATE_RANK)])
    cols = [_pad_heads(gq, GLA_HEADS, GLA_DK), _pad_heads(gk, GLA_HEADS, GLA_DK), gv, g_out,
            hq, hf, hi, h_out, qc, kvc, kpe, _rot_cols(kpe), g_low]
    out = jnp.concatenate(cols, axis=-1).astype(BF16)
    assert out.shape[-1] == PROJ_COLS
    return out


def _layout_wq(wq_b):
    lead = wq_b.shape[:-1]
    w = wq_b.reshape(lead + (MLA_HEADS, MLA_NOPE + MLA_ROPE))
    nope, pe = w[..., :MLA_NOPE], w[..., MLA_NOPE:]
    w = jnp.concatenate([nope, pe, _rot_cols(pe)], axis=-1)
    return w.reshape(lead + (MLA_HEADS * 2 * LANE,)).astype(BF16)


def _layout_wkv(wkv_b):
    lead = wkv_b.shape[:-1]
    w = wkv_b.reshape(lead + (MLA_HEADS, MLA_NOPE + MLA_DV))
    k = w[..., :MLA_NOPE].reshape(lead + (MLA_HEADS * MLA_NOPE,))
    v = w[..., MLA_NOPE:].reshape(lead + (MLA_HEADS * MLA_DV,))
    return k.astype(BF16), jnp.swapaxes(v, -1, -2).astype(BF16)


def kernel(x, positions, attn_pre_norm, w_in, gla_gate_w2, gla_gate_b, gla_out_norm, hgrn_lb_logits,
           hgrn_out_norm, mla_q_norm, mla_wq_b, mla_kv_norm, mla_wkv_b, mla_out_norm, w_out,
           attn_post_norm, ffn_pre_norm, w_gate, w_up, w_down, ffn_post_norm):
    depth = w_in.shape[0]
    inv_freq = ROPE_THETA ** (-jnp.arange(0, MLA_ROPE, 2, dtype=F32) / MLA_ROPE)
    ang = positions.astype(F32).reshape(TOKENS, 1) * inv_freq
    cs = jnp.concatenate([jnp.cos(ang), jnp.cos(ang), jnp.sin(ang), jnp.sin(ang)], axis=-1)
    csum = jnp.cumsum(jax.nn.softmax(hgrn_lb_logits.astype(F32), axis=0), axis=0)
    lb = csum - csum[0]
    log_lb = jnp.log(lb)
    log_1m_lb = jnp.log1p(-lb)

    w_in_l = _layout_w_in(w_in)
    wq_l = _layout_wq(mla_wq_b)
    wk_l, wvt_l = _layout_wkv(mla_wkv_b)
    w_out_l = w_out.astype(BF16)
    w_gate_l, w_up_l, w_down_l = w_gate.astype(BF16), w_up.astype(BF16), w_down.astype(BF16)
    w2 = jnp.pad(_pad_heads(gla_gate_w2, GLA_HEADS, GLA_DK), [(0, 0), (0, LANE - GLA_GATE_RANK), (0, 0)])
    b2 = _pad_heads(gla_gate_b, GLA_HEADS, GLA_DK)
    sums = jnp.asarray(_decay_sum_matrix(), dtype=BF16)

    def row(p, l):
        return p[l].reshape(1, -1)

    xf = x.reshape(TOKENS, D_MODEL)
    for l in range(depth):
        proj = _norm_matmul(xf, row(attn_pre_norm, l), w_in_l, l, F32)
        y_gla = _rec_call(
            "gla", proj,
            [(CB_GQ, 4), (CB_GK, 4), (CB_GV, 4), (CB_GLOW, 1), (CB_GOUT, 4)],
            [w2[l], row(b2, l), row(gla_out_norm, l), sums])
        y_hg = _rec_call(
            "hgrn", proj,
            [(CB_HQ, 4), (CB_HF, 4), (CB_HI, 4), (CB_HOUT, 4)],
            [row(log_lb, l), row(log_1m_lb, l), row(hgrn_out_norm, l), sums])
        q, kn, vt, kr = _mla_proj(proj, cs, row(mla_q_norm, l), row(mla_kv_norm, l), wq_l, wk_l, wvt_l, l)
        o_mla = _attention(q, kn, kr, vt)
        xf = _out_proj(y_gla, y_hg, o_mla, xf, row(mla_out_norm, l), w_out_l, l, row(attn_post_norm, l))
        xf = _ffn(xf, row(ffn_pre_norm, l), w_gate_l, w_up_l, w_down_l, l, row(ffn_post_norm, l))
    return xf.reshape(x.shape)
```

```python
import functools

import numpy as np
import jax
import jax.numpy as jnp
from jax import lax
from jax.experimental import pallas as pl
from jax.experimental.pallas import tpu as pltpu

F32 = jnp.float32
BF16 = jnp.bfloat16

D_MODEL = 2048
BATCH = 2
SEQ = 4096
DEPTH = 4
TOKENS = BATCH * SEQ
GLA_HEADS = 4
GLA_DK = 64
GLA_DV = 128
GLA_GATE_RANK = 16
GLA_GATE_NORMALIZER = 16.0
HGRN_HEADS = 4
HGRN_DK = 128
HGRN_DV = 128
MLA_HEADS = 8
MLA_Q_RANK = 512
MLA_KV_RANK = 512
MLA_NOPE = 128
MLA_ROPE = 64
MLA_DV = 128
ROPE_THETA = 10000.0
D_FF = -(-8 * D_MODEL // (3 * 256)) * 256
EPS = 1e-6
LOG2_E = float(np.log2(np.e))

LANE = 128
VMEM_LIMIT = 56 * 1024 * 1024

CB_GQ, CB_GK, CB_GV, CB_GOUT = 0, 4, 8, 12
CB_HQ, CB_HF, CB_HI, CB_HOUT = 16, 20, 24, 28
CB_QC, CB_KVC = 32, 36
CB_KPE, CB_GLOW = 40, 41
PROJ_COLS = 42 * LANE

PROJ_TM, PROJ_TN = 1024, 768
REC_HEADS = 4
REC_CHUNK = 128
REC_TB = 256
REC_LEVELS = tuple(2 ** p for p in range(1, 8))
MLA_TM = 512
ATT_TQ = 512
ATT_HEADS = 4
VT_ROWS = MLA_DV + 16
OUT_TM = 256
FFN_TM, FFN_TF = 1024, 256


def _cparams(*sem):
    return pltpu.CompilerParams(dimension_semantics=sem, vmem_limit_bytes=VMEM_LIMIT)


def _dot(a, b):
    return jnp.dot(a, b, preferred_element_type=F32)


def _dot_nt(a, b):
    return lax.dot_general(a, b, (((1,), (1,)), ((), ())), preferred_element_type=F32)


def _dot_tn(a, b):
    return lax.dot_general(a, b, (((0,), (0,)), ((), ())), preferred_element_type=F32)


def _split_bf16(x):
    hi = x.astype(BF16)
    lo = (x - hi.astype(F32)).astype(BF16)
    return hi, lo


def _rms(x, gain):
    ms = jnp.mean(x * x, axis=-1, keepdims=True)
    return x * lax.rsqrt(ms + EPS) * gain


def _sigmoid(x):
    return 1.0 / (1.0 + jnp.exp(-x))


def _softplus_neg_abs(x):
    return jnp.log(1.0 + jnp.exp(-jnp.abs(x)))


def _log_sigmoid(x):
    return jnp.minimum(x, 0.0) - _softplus_neg_abs(x)


def _layer_spec(block, index_map, layer, **kwargs):
    return pl.BlockSpec((None,) + block, lambda *g: (layer,) + index_map(*g), **kwargs)


def _norm_matmul_kernel(x_ref, g_ref, w_ref, o_ref, h_ref):
    @pl.when(pl.program_id(1) == 0)
    def _():
        h_ref[...] = _rms(x_ref[...], g_ref[...]).astype(BF16)

    o_ref[...] = _dot(h_ref[...], w_ref[...]).astype(o_ref.dtype)


def _norm_matmul(x, gain, w, layer, out_dtype):
    m, d = x.shape
    n = w.shape[-1]
    return pl.pallas_call(
        _norm_matmul_kernel,
        grid=(m // PROJ_TM, n // PROJ_TN),
        in_specs=[
            pl.BlockSpec((PROJ_TM, d), lambda i, j: (i, 0)),
            pl.BlockSpec((1, d), lambda i, j: (0, 0)),
            _layer_spec((d, PROJ_TN), lambda i, j: (0, j), layer),
        ],
        out_specs=pl.BlockSpec((PROJ_TM, PROJ_TN), lambda i, j: (i, j)),
        out_shape=jax.ShapeDtypeStruct((m, n), out_dtype),
        scratch_shapes=[pltpu.VMEM((PROJ_TM, d), BF16)],
        compiler_params=_cparams("parallel", "arbitrary"),
        name="norm_proj",
    )(x, gain, w)


def _decay_sum_matrix():
    c = REC_CHUNK
    t = np.arange(c)
    blocks = [(t[None, :] <= t[:, None])]
    for m in REC_LEVELS:
        half = m // 2
        pos = t % m
        mid = t - pos + half
        upper = (pos >= half)[:, None] & (t[None, :] >= mid[:, None]) & (t[None, :] <= t[:, None])
        lower = (pos < half)[:, None] & (t[None, :] > t[:, None]) & (t[None, :] < mid[:, None])
        blocks.append(upper | lower)
    blocks.append(t[None, :] > t[:, None])
    sums = np.concatenate(blocks, axis=0).astype(np.float32)
    return np.concatenate([sums, sums], axis=1)


def _pair_level_codes():
    c = REC_CHUNK
    ri = lax.broadcasted_iota(jnp.int32, (c, c), 0)
    ci = lax.broadcasted_iota(jnp.int32, (c, c), 1)
    lvl = 32 - lax.clz(ri ^ ci)
    code = jnp.where(ri > ci, lvl, jnp.where(ri == ci, 0, -1))
    return code.astype(F32).astype(BF16)


def _rec_finish(o, og, gain):
    return _rms(o, gain) * (og * _sigmoid(og))


def _rec_kernel(kind, *refs):
    if kind == "gla":
        q_ref, k_ref, v_ref, gl_ref, og_ref, w2_ref, b2_ref, gain_ref, sums_ref, y_ref, st_ref = refs
    else:
        q_ref, f_ref, v_ref, og_ref, llb_ref, l1m_ref, gain_ref, sums_ref, y_ref, st_ref = refs

    @pl.when(pl.program_id(1) == 0)
    def _():
        st_ref[...] = jnp.zeros_like(st_ref)

    c = REC_CHUNK
    nl = len(REC_LEVELS)
    sums = sums_ref[...]
    gain = gain_ref[...]
    code = _pair_level_codes()
    chunks = [slice(ck * c, (ck + 1) * c) for ck in range(REC_TB // c)]
    lanes = [slice(h * LANE, (h + 1) * LANE) for h in range(REC_HEADS)]
    pairs = [(ck, h) for ck in range(len(chunks)) for h in range(REC_HEADS)]
    if kind == "gla":
        w2_hi, w2_lo = _split_bf16(w2_ref[...])

    qs, ks, es = [], [], []
    for sl in chunks:
        if kind == "gla":
            gl_hi, gl_lo = _split_bf16(gl_ref[sl, :])
            z = _dot(gl_hi, w2_hi) + _dot(gl_lo, w2_hi) + _dot(gl_hi, w2_lo) + b2_ref[...]
            g = _log_sigmoid(z) * (1.0 / GLA_GATE_NORMALIZER)
            q = q_ref[sl, :] * (GLA_DK ** -0.5)
            k = k_ref[sl, :]
        else:
            hq = q_ref[sl, :]
            q = hq * _sigmoid(hq)
            lo = llb_ref[...]
            hi = l1m_ref[...] + _log_sigmoid(f_ref[sl, :])
            g = jnp.maximum(lo, hi) + _softplus_neg_abs(lo - hi)
            k = 1.0 - jnp.exp(g)
        g_hi, g_lo = _split_bf16(g * LOG2_E)
        es.append(jnp.exp2(_dot(sums, jnp.concatenate([g_hi, g_lo], axis=0))))
        qs.append(q)
        ks.append(k)

    q16 = [q.astype(BF16) for q in qs]
    k16 = [k.astype(BF16) for k in ks]
    raw = {(ck, h): _dot_nt(q16[ck][:, lanes[h]], k16[ck][:, lanes[h]]) for ck, h in pairs}
    a16 = {p: jnp.where(code == 0, raw[p].astype(BF16), jnp.zeros((c, c), BF16)) for p in pairs}
    for lvl in range(1, nl + 1):
        e16 = [e[lvl * c:(lvl + 1) * c].astype(BF16) for e in es]
        qm = [q16[ck] * e16[ck] for ck in range(len(chunks))]
        km = [k16[ck] * e16[ck] for ck in range(len(chunks))]
        raw = {(ck, h): _dot_nt(qm[ck][:, lanes[h]], km[ck][:, lanes[h]]) for ck, h in pairs}
        a16 = {p: jnp.where(code == lvl, raw[p].astype(BF16), a16[p]) for p in pairs}

    st = [st_ref[h] for h in range(REC_HEADS)]
    for ck, sl in enumerate(chunks):
        e_cum = es[ck][0:c]
        e_tail = es[ck][(nl + 1) * c:(nl + 2) * c]
        q_in = (qs[ck] * e_cum).astype(BF16)
        k_out = (ks[ck] * e_tail).astype(BF16)
        v16 = v_ref[sl, :].astype(BF16)
        for h in range(REC_HEADS):
            ls = lanes[h]
            o = _dot(a16[ck, h], v16[:, ls]) + _dot_nt(q_in[:, ls], st[h].astype(BF16))
            st[h] = st[h] * e_cum[c - 1:c, ls] + _dot_tn(v16[:, ls], k_out[:, ls])
            y_ref[sl, ls] = _rec_finish(o, og_ref[sl, ls], gain).astype(y_ref.dtype)
    for h in range(REC_HEADS):
        st_ref[h] = st[h]


def _rec_call(kind, proj, col_blocks, params):
    nt = SEQ // REC_TB
    in_specs, args = [], []
    for cb, width in col_blocks:
        in_specs.append(pl.BlockSpec((REC_TB, width * LANE),
                                     lambda b, t, cb=cb, width=width: (b * nt + t, cb // width)))
        args.append(proj)
    for p in params:
        in_specs.append(pl.BlockSpec(p.shape, lambda b, t: (0, 0)))
        args.append(p)
    wide = REC_HEADS * LANE
    return pl.pallas_call(
        functools.partial(_rec_kernel, kind),
        grid=(BATCH, nt),
        in_specs=in_specs,
        out_specs=pl.BlockSpec((REC_TB, wide), lambda b, t: (b * nt + t, 0)),
        out_shape=jax.ShapeDtypeStruct((TOKENS, wide), BF16),
        scratch_shapes=[pltpu.VMEM((REC_HEADS, LANE, LANE), F32)],
        compiler_params=_cparams("parallel", "arbitrary"),
        name=kind + "_recurrence",
    )(*args)


def _mla_proj_kernel(qc_ref, kvc_ref, kpe_ref, cs_ref, qn_ref, kvn_ref, wq_ref, wk_ref, wvt_ref,
                     q_ref, kn_ref, vt_ref, kr_ref):
    scale = (MLA_NOPE + MLA_ROPE) ** -0.5
    cs = cs_ref[...]
    qn = _rms(qc_ref[...], qn_ref[...]).astype(BF16)
    q = _dot(qn, wq_ref[...])
    for h in range(MLA_HEADS):
        lo = h * 2 * LANE
        q_ref[:, lo:lo + LANE] = (q[:, lo:lo + LANE] * scale).astype(q_ref.dtype)
        q_ref[:, lo + LANE:lo + 2 * LANE] = (q[:, lo + LANE:lo + 2 * LANE] * (cs * scale)).astype(q_ref.dtype)
    kvn = _rms(kvc_ref[...], kvn_ref[...]).astype(BF16)
    kn_ref[...] = _dot(kvn, wk_ref[...]).astype(kn_ref.dtype)
    vt = _dot_nt(wvt_ref[...], kvn)
    ones = jnp.ones((VT_ROWS - MLA_DV, vt.shape[1]), vt_ref.dtype)
    for h in range(MLA_HEADS):
        vt_ref[h * VT_ROWS:h * VT_ROWS + MLA_DV, :] = vt[h * MLA_DV:(h + 1) * MLA_DV].astype(vt_ref.dtype)
        vt_ref[h * VT_ROWS + MLA_DV:(h + 1) * VT_ROWS, :] = ones
    t = kpe_ref[...] * cs
    kr_ref[...] = (t + pltpu.roll(t, LANE // 2, 1)).astype(kr_ref.dtype)


def _mla_proj(proj, cs, qn_gain, kvn_gain, wq, wk, wvt, layer):
    tm = MLA_TM
    wide = 4 * LANE
    nq = MLA_HEADS * 2 * LANE
    nk = MLA_HEADS * MLA_NOPE
    nvt = MLA_HEADS * VT_ROWS
    return pl.pallas_call(
        _mla_proj_kernel,
        grid=(TOKENS // tm,),
        in_specs=[
            pl.BlockSpec((tm, wide), lambda i: (i, CB_QC // 4)),
            pl.BlockSpec((tm, wide), lambda i: (i, CB_KVC // 4)),
            pl.BlockSpec((tm, LANE), lambda i: (i, CB_KPE)),
            pl.BlockSpec((tm, LANE), lambda i: (i, 0)),
            pl.BlockSpec((1, MLA_Q_RANK), lambda i: (0, 0)),
            pl.BlockSpec((1, MLA_KV_RANK), lambda i: (0, 0)),
            _layer_spec(wq.shape[1:], lambda i: (0, 0), layer),
            _layer_spec(wk.shape[1:], lambda i: (0, 0), layer),
            _layer_spec(wvt.shape[1:], lambda i: (0, 0), layer),
        ],
        out_specs=[
            pl.BlockSpec((tm, nq), lambda i: (i, 0)),
            pl.BlockSpec((tm, nk), lambda i: (i, 0)),
            pl.BlockSpec((nvt, tm), lambda i: (0, i)),
            pl.BlockSpec((tm, LANE), lambda i: (i, 0)),
        ],
        out_shape=[
            jax.ShapeDtypeStruct((TOKENS, nq), BF16),
            jax.ShapeDtypeStruct((TOKENS, nk), BF16),
            jax.ShapeDtypeStruct((nvt, TOKENS), BF16),
            jax.ShapeDtypeStruct((TOKENS, LANE), BF16),
        ],
        compiler_params=_cparams("parallel"),
        name="mla_proj",
    )(proj, proj, proj, cs, qn_gain, kvn_gain, wq, wk, wvt)


def _attn_kernel(q_ref, kn_ref, kr_ref, vt_ref, o_ref):
    tq = ATT_TQ
    qi = pl.program_id(2)
    qs = [q_ref[:, h * 2 * LANE:(h + 1) * 2 * LANE] for h in range(ATT_HEADS)]

    def step(j, state, masked):
        start = pl.multiple_of(j * tq, tq)
        kr = kr_ref[pl.ds(start, tq), :]
        s_blocks = []
        for h in range(ATT_HEADS):
            k = jnp.concatenate([kn_ref[pl.ds(start, tq), h * LANE:(h + 1) * LANE], kr], axis=1)
            s_blocks.append(_dot_nt(k, qs[h]))
        out = []
        for h in range(ATT_HEADS):
            m, acc = state[h]
            s = s_blocks[h]
            if masked:
                ki = lax.broadcasted_iota(jnp.int32, s.shape, 0)
                qj = lax.broadcasted_iota(jnp.int32, s.shape, 1)
                s = jnp.where(ki <= qj, s, -jnp.inf)
            m_new = jnp.maximum(m, jnp.max(s, axis=0, keepdims=True))
            alpha = jnp.exp(m - m_new)
            p = jnp.exp((s - m_new).astype(BF16))
            acc = alpha * acc + _dot(vt_ref[h * VT_ROWS:(h + 1) * VT_ROWS, pl.ds(start, tq)], p)
            out.append((m_new, acc))
        return tuple(out)

    init = tuple((jnp.full((1, tq), -jnp.inf, F32), jnp.zeros((VT_ROWS, tq), F32)) for _ in range(ATT_HEADS))
    state = lax.fori_loop(0, qi, functools.partial(step, masked=False), init)
    final = step(qi, state, masked=True)
    for h in range(ATT_HEADS):
        _, acc = final[h]
        o = acc[:MLA_DV] / acc[MLA_DV:MLA_DV + 1]
        o_ref[:, h * LANE:(h + 1) * LANE] = o.T.astype(o_ref.dtype)


def _attention(q, kn, kr, vt):
    tq = ATT_TQ
    nq = SEQ // tq
    hw = ATT_HEADS * LANE
    return pl.pallas_call(
        _attn_kernel,
        grid=(BATCH, MLA_HEADS // ATT_HEADS, nq),
        in_specs=[
            pl.BlockSpec((tq, 2 * hw), lambda b, h, i: (b * nq + i, h)),
            pl.BlockSpec((SEQ, hw), lambda b, h, i: (b, h)),
            pl.BlockSpec((SEQ, LANE), lambda b, h, i: (b, 0)),
            pl.BlockSpec((ATT_HEADS * VT_ROWS, SEQ), lambda b, h, i: (h, b)),
        ],
        out_specs=pl.BlockSpec((tq, hw), lambda b, h, i: (b * nq + i, h)),
        out_shape=jax.ShapeDtypeStruct((TOKENS, MLA_HEADS * MLA_DV), F32),
        compiler_params=_cparams("parallel", "parallel", "arbitrary"),
        name="mla_attention",
    )(q, kn, kr, vt)


def _out_proj_kernel(yg_ref, yh_ref, om_ref, x_ref, mg_ref, w_ref, pg_ref, o_ref, w16_ref):
    @pl.when(pl.program_id(0) == 0)
    def _():
        w16_ref[...] = w_ref[...].astype(BF16)

    ym = _rms(om_ref[...], mg_ref[...]).astype(BF16)
    mix = jnp.concatenate([yg_ref[...], yh_ref[...], ym], axis=1)
    m = _dot(mix, w16_ref[...])
    o_ref[...] = x_ref[...] + _rms(m, pg_ref[...])


def _out_proj(y_gla, y_hg, o_mla, x, mla_gain, w_out, layer, post_gain):
    tm = OUT_TM
    d = D_MODEL
    return pl.pallas_call(
        _out_proj_kernel,
        grid=(TOKENS // tm,),
        in_specs=[
            pl.BlockSpec((tm, y_gla.shape[1]), lambda i: (i, 0)),
            pl.BlockSpec((tm, y_hg.shape[1]), lambda i: (i, 0)),
            pl.BlockSpec((tm, o_mla.shape[1]), lambda i: (i, 0)),
            pl.BlockSpec((tm, d), lambda i: (i, 0)),
            pl.BlockSpec((1, o_mla.shape[1]), lambda i: (0, 0)),
            _layer_spec(w_out.shape[1:], lambda i: (0, 0), layer, pipeline_mode=pl.Buffered(1)),
            pl.BlockSpec((1, d), lambda i: (0, 0)),
        ],
        out_specs=pl.BlockSpec((tm, d), lambda i: (i, 0)),
        out_shape=jax.ShapeDtypeStruct((TOKENS, d), F32),
        scratch_shapes=[pltpu.VMEM(w_out.shape[1:], BF16)],
        compiler_params=_cparams("arbitrary"),
        name="out_proj",
    )(y_gla, y_hg, o_mla, x, mla_gain, w_out, post_gain)


def _ffn_kernel(x_ref, ug_ref, wg_ref, wu_ref, wd_ref, pg_ref, o_ref, u_ref):
    j = pl.program_id(1)

    @pl.when(j == 0)
    def _():
        u_ref[...] = _rms(x_ref[...], ug_ref[...]).astype(BF16)
        o_ref[...] = jnp.zeros_like(o_ref)

    u = u_ref[...]
    gate = _dot(u, wg_ref[...].astype(BF16))
    up = _dot(u, wu_ref[...].astype(BF16))
    hid = (gate * _sigmoid(gate) * up).astype(BF16)
    o_ref[...] += _dot(hid, wd_ref[...].astype(BF16))

    @pl.when(j == pl.num_programs(1) - 1)
    def _():
        o_ref[...] = x_ref[...] + _rms(o_ref[...], pg_ref[...])


def _ffn(x, pre_gain, wg, wu, wd, layer, post_gain):
    tm, tf = FFN_TM, FFN_TF
    d = D_MODEL
    once = pl.Buffered(1)
    return pl.pallas_call(
        _ffn_kernel,
        grid=(TOKENS // tm, D_FF // tf),
        in_specs=[
            pl.BlockSpec((tm, d), lambda i, j: (i, 0), pipeline_mode=once),
            pl.BlockSpec((1, d), lambda i, j: (0, 0)),
            _layer_spec((d, tf), lambda i, j: (0, j), layer),
            _layer_spec((d, tf), lambda i, j: (0, j), layer),
            _layer_spec((tf, d), lambda i, j: (j, 0), layer),
            pl.BlockSpec((1, d), lambda i, j: (0, 0)),
        ],
        out_specs=pl.BlockSpec((tm, d), lambda i, j: (i, 0), pipeline_mode=once),
        out_shape=jax.ShapeDtypeStruct((TOKENS, d), F32),
        scratch_shapes=[pltpu.VMEM((tm, d), BF16)],
        compiler_params=_cparams("parallel", "arbitrary"),
        name="ffn",
    )(x, pre_gain, wg, wu, wd, post_gain)


def _rot_cols(w):
    half = MLA_ROPE // 2
    return jnp.concatenate([-w[..., half:], w[..., :half]], axis=-1)


def _pad_heads(w, heads, width):
    lead = w.shape[:-1]
    w = w.reshape(lead + (heads, width))
    w = jnp.pad(w, [(0, 0)] * len(lead) + [(0, 0), (0, LANE - width)])
    return w.reshape(lead + (heads * LANE,))


W_IN_SIZES = (GLA_HEADS * GLA_DK, GLA_HEADS * GLA_DK, GLA_HEADS * GLA_DV, GLA_GATE_RANK,
              GLA_HEADS * GLA_DV, HGRN_HEADS * HGRN_DK, HGRN_HEADS * HGRN_DK, HGRN_HEADS * HGRN_DV,
              HGRN_HEADS * HGRN_DV, MLA_Q_RANK, MLA_KV_RANK, MLA_ROPE)
W_IN_ROWS = 256


def _layout_w_in_kernel(w_ref, o_ref):
    w = w_ref[...]
    rows = w.shape[0]
    off = [int(c) for c in np.cumsum((0,) + W_IN_SIZES)]
    gq, gk, gv, g_low, g_out, hq, _, _, _, qc, _, kpe = off[:12]

    def put(cb, block):
        o_ref[:, cb * LANE:cb * LANE + block.shape[1]] = block.astype(o_ref.dtype)

    pad_dk = jnp.zeros((rows, LANE - GLA_DK), F32)
    for h in range(GLA_HEADS):
        put(CB_GQ + h, jnp.concatenate([w[:, gq + h * GLA_DK:gq + (h + 1) * GLA_DK], pad_dk], axis=1))
        put(CB_GK + h, jnp.concatenate([w[:, gk + h * GLA_DK:gk + (h + 1) * GLA_DK], pad_dk], axis=1))
    put(CB_GV, w[:, gv:g_low])
    put(CB_GOUT, w[:, g_out:hq])
    put(CB_HQ, w[:, hq:qc])
    put(CB_QC, w[:, qc:kpe])
    pe = w[:, kpe:kpe + MLA_ROPE]
    half = MLA_ROPE // 2
    put(CB_KPE, jnp.concatenate([pe, -pe[:, half:], pe[:, :half]], axis=1))
    put(CB_GLOW, jnp.concatenate([w[:, g_low:g_out], jnp.zeros((rows, LANE - GLA_GATE_RANK), F32)], axis=1))


def _layout_w_in(w_in):
    depth, d, n = w_in.shape
    return pl.pallas_call(
        _layout_w_in_kernel,
        grid=(depth, d // W_IN_ROWS),
        in_specs=[pl.BlockSpec((None, W_IN_ROWS, n), lambda l, i: (l, i, 0))],
        out_specs=pl.BlockSpec((None, W_IN_ROWS, PROJ_COLS), lambda l, i: (l, i, 0)),
        out_shape=jax.ShapeDtypeStruct((depth, d, PROJ_COLS), BF16),
        compiler_params=_cparams("parallel", "parallel"),
        name="layout_w_in",
    )(w_in)


def _layout_wq(wq_b):
    lead = wq_b.shape[:-1]
    w = wq_b.reshape(lead + (MLA_HEADS, MLA_NOPE + MLA_ROPE))
    nope, pe = w[..., :MLA_NOPE], w[..., MLA_NOPE:]
    w = jnp.concatenate([nope, pe, _rot_cols(pe)], axis=-1)
    return w.reshape(lead + (MLA_HEADS * 2 * LANE,)).astype(BF16)


def _layout_wkv(wkv_b):
    lead = wkv_b.shape[:-1]
    w = wkv_b.reshape(lead + (MLA_HEADS, MLA_NOPE + MLA_DV))
    k = w[..., :MLA_NOPE].reshape(lead + (MLA_HEADS * MLA_NOPE,))
    v = w[..., MLA_NOPE:].reshape(lead + (MLA_HEADS * MLA_DV,))
    return k.astype(BF16), jnp.swapaxes(v, -1, -2).astype(BF16)


def kernel(x, positions, attn_pre_norm, w_in, gla_gate_w2, gla_gate_b, gla_out_norm, hgrn_lb_logits,
           hgrn_out_norm, mla_q_norm, mla_wq_b, mla_kv_norm, mla_wkv_b, mla_out_norm, w_out,
           attn_post_norm, ffn_pre_norm, w_gate, w_up, w_down, ffn_post_norm):
    depth = w_in.shape[0]
    inv_freq = ROPE_THETA ** (-jnp.arange(0, MLA_ROPE, 2, dtype=F32) / MLA_ROPE)
    ang = positions.astype(F32).reshape(TOKENS, 1) * inv_freq
    cs = jnp.concatenate([jnp.cos(ang), jnp.cos(ang), jnp.sin(ang), jnp.sin(ang)], axis=-1)
    csum = jnp.cumsum(jax.nn.softmax(hgrn_lb_logits.astype(F32), axis=0), axis=0)
    lb = csum - csum[0]
    log_lb = jnp.log(lb)
    log_1m_lb = jnp.log1p(-lb)

    w_in_l = _layout_w_in(w_in)
    wq_l = _layout_wq(mla_wq_b)
    wk_l, wvt_l = _layout_wkv(mla_wkv_b)
    w2 = jnp.pad(_pad_heads(gla_gate_w2, GLA_HEADS, GLA_DK), [(0, 0), (0, LANE - GLA_GATE_RANK), (0, 0)])
    b2 = _pad_heads(gla_gate_b, GLA_HEADS, GLA_DK)
    sums = jnp.asarray(_decay_sum_matrix(), dtype=BF16)

    def row(p, l):
        return p[l].reshape(1, -1)

    xf = x.reshape(TOKENS, D_MODEL)
    for l in range(depth):
        proj = _norm_matmul(xf, row(attn_pre_norm, l), w_in_l, l, F32)
        y_gla = _rec_call(
            "gla", proj,
            [(CB_GQ, 4), (CB_GK, 4), (CB_GV, 4), (CB_GLOW, 1), (CB_GOUT, 4)],
            [w2[l], row(b2, l), row(gla_out_norm, l), sums])
        y_hg = _rec_call(
            "hgrn", proj,
            [(CB_HQ, 4), (CB_HF, 4), (CB_HI, 4), (CB_HOUT, 4)],
            [row(log_lb, l), row(log_1m_lb, l), row(hgrn_out_norm, l), sums])
        q, kn, vt, kr = _mla_proj(proj, cs, row(mla_q_norm, l), row(mla_kv_norm, l), wq_l, wk_l, wvt_l, l)
        o_mla = _attention(q, kn, kr, vt)
        xf = _out_proj(y_gla, y_hg, o_mla, xf, row(mla_out_norm, l), w_out, l, row(attn_post_norm, l))
        xf = _ffn(xf, row(ffn_pre_norm, l), w_gate, w_up, w_down, l, row(ffn_post_norm, l))
    return xf.reshape(x.shape)
```

```python
import functools

import numpy as np
import jax
import jax.numpy as jnp
from jax import lax
from jax.experimental import pallas as pl
from jax.experimental.pallas import tpu as pltpu

F32 = jnp.float32
BF16 = jnp.bfloat16

D_MODEL = 2048
BATCH = 2
SEQ = 4096
DEPTH = 4
TOKENS = BATCH * SEQ
GLA_HEADS = 4
GLA_DK = 64
GLA_DV = 128
GLA_GATE_RANK = 16
GLA_GATE_NORMALIZER = 16.0
HGRN_HEADS = 4
HGRN_DK = 128
HGRN_DV = 128
MLA_HEADS = 8
MLA_Q_RANK = 512
MLA_KV_RANK = 512
MLA_NOPE = 128
MLA_ROPE = 64
MLA_DV = 128
ROPE_THETA = 10000.0
D_FF = -(-8 * D_MODEL // (3 * 256)) * 256
EPS = 1e-6
LOG2_E = float(np.log2(np.e))

LANE = 128
VMEM_LIMIT = 56 * 1024 * 1024

CB_GQ, CB_GK, CB_GV, CB_GOUT = 0, 4, 8, 12
CB_HQ, CB_HF, CB_HI, CB_HOUT = 16, 20, 24, 28
CB_QC, CB_KVC = 32, 36
CB_KPE, CB_GLOW = 40, 41
PROJ_COLS = 42 * LANE

PROJ_TM, PROJ_TN = 1024, 1792
REC_HEADS = 4
REC_CHUNK = 128
REC_TB = 512
REC_LEVELS = tuple(2 ** p for p in range(1, 8))
MLA_TM = 512
ATT_TQ = 512
ATT_HEADS = 4
VT_ROWS = MLA_DV + 16
OUT_TM = 512
FFN_TM, FFN_TF = 1024, 256


def _cparams(*sem):
    return pltpu.CompilerParams(dimension_semantics=sem, vmem_limit_bytes=VMEM_LIMIT)


def _dot(a, b):
    return jnp.dot(a, b, preferred_element_type=F32)


def _dot_nt(a, b):
    return lax.dot_general(a, b, (((1,), (1,)), ((), ())), preferred_element_type=F32)


def _dot_tn(a, b):
    return lax.dot_general(a, b, (((0,), (0,)), ((), ())), preferred_element_type=F32)


def _split_bf16(x):
    hi = x.astype(BF16)
    lo = (x - hi.astype(F32)).astype(BF16)
    return hi, lo


def _rms(x, gain):
    ms = jnp.mean(x * x, axis=-1, keepdims=True)
    return x * lax.rsqrt(ms + EPS) * gain


def _sigmoid(x):
    return 1.0 / (1.0 + jnp.exp(-x))


def _softplus_neg_abs(x):
    return jnp.log(1.0 + jnp.exp(-jnp.abs(x)))


def _log_sigmoid(x):
    return jnp.minimum(x, 0.0) - _softplus_neg_abs(x)


def _layer_spec(block, index_map, layer, **kwargs):
    return pl.BlockSpec((None,) + block, lambda *g: (layer,) + index_map(*g), **kwargs)


def _norm_matmul_kernel(x_ref, g_ref, w_ref, o_ref, h_ref):
    @pl.when(pl.program_id(1) == 0)
    def _():
        h_ref[...] = _rms(x_ref[...], g_ref[...]).astype(BF16)

    o_ref[...] = _dot(h_ref[...], w_ref[...]).astype(o_ref.dtype)


def _norm_matmul(x, gain, w, layer, out_dtype):
    m, d = x.shape
    n = w.shape[-1]
    return pl.pallas_call(
        _norm_matmul_kernel,
        grid=(m // PROJ_TM, n // PROJ_TN),
        in_specs=[
            pl.BlockSpec((PROJ_TM, d), lambda i, j: (i, 0)),
            pl.BlockSpec((1, d), lambda i, j: (0, 0)),
            _layer_spec((d, PROJ_TN), lambda i, j: (0, j), layer),
        ],
        out_specs=pl.BlockSpec((PROJ_TM, PROJ_TN), lambda i, j: (i, j)),
        out_shape=jax.ShapeDtypeStruct((m, n), out_dtype),
        scratch_shapes=[pltpu.VMEM((PROJ_TM, d), BF16)],
        compiler_params=_cparams("parallel", "arbitrary"),
        name="norm_proj",
    )(x, gain, w)


def _decay_sum_matrix():
    c = REC_CHUNK
    t = np.arange(c)
    blocks = [(t[None, :] <= t[:, None])]
    for m in REC_LEVELS:
        half = m // 2
        pos = t % m
        mid = t - pos + half
        upper = (pos >= half)[:, None] & (t[None, :] >= mid[:, None]) & (t[None, :] <= t[:, None])
        lower = (pos < half)[:, None] & (t[None, :] > t[:, None]) & (t[None, :] < mid[:, None])
        blocks.append(upper | lower)
    blocks.append(t[None, :] > t[:, None])
    sums = np.concatenate(blocks, axis=0).astype(np.float32)
    return np.concatenate([sums, sums], axis=1)


def _pair_level_codes():
    c = REC_CHUNK
    ri = lax.broadcasted_iota(jnp.int32, (c, c), 0)
    ci = lax.broadcasted_iota(jnp.int32, (c, c), 1)
    lvl = 32 - lax.clz(ri ^ ci)
    code = jnp.where(ri > ci, lvl, jnp.where(ri == ci, 0, -1))
    return code.astype(F32).astype(BF16)


def _rec_finish(o, og, gain):
    return _rms(o, gain) * (og * _sigmoid(og))


def _rec_kernel(kind, *refs):
    if kind == "gla":
        q_ref, k_ref, v_ref, gl_ref, og_ref, w2_ref, b2_ref, gain_ref, sums_ref, y_ref, st_ref = refs
    else:
        q_ref, f_ref, v_ref, og_ref, llb_ref, l1m_ref, gain_ref, sums_ref, y_ref, st_ref = refs

    @pl.when(pl.program_id(1) == 0)
    def _():
        st_ref[...] = jnp.zeros_like(st_ref)

    c = REC_CHUNK
    nl = len(REC_LEVELS)
    sums = sums_ref[...]
    gain = gain_ref[...]
    code = _pair_level_codes()
    chunks = [slice(ck * c, (ck + 1) * c) for ck in range(REC_TB // c)]
    lanes = [slice(h * LANE, (h + 1) * LANE) for h in range(REC_HEADS)]
    pairs = [(ck, h) for ck in range(len(chunks)) for h in range(REC_HEADS)]
    if kind == "gla":
        w2_hi, w2_lo = _split_bf16(w2_ref[...])

    qs, ks, es = [], [], []
    for sl in chunks:
        if kind == "gla":
            gl_hi, gl_lo = _split_bf16(gl_ref[sl, :])
            z = _dot(gl_hi, w2_hi) + _dot(gl_lo, w2_hi) + _dot(gl_hi, w2_lo) + b2_ref[...]
            g = _log_sigmoid(z) * (1.0 / GLA_GATE_NORMALIZER)
            q = q_ref[sl, :] * (GLA_DK ** -0.5)
            k = k_ref[sl, :]
        else:
            hq = q_ref[sl, :]
            q = hq * _sigmoid(hq)
            lo = llb_ref[...]
            hi = l1m_ref[...] + _log_sigmoid(f_ref[sl, :])
            g = jnp.maximum(lo, hi) + _softplus_neg_abs(lo - hi)
            k = 1.0 - jnp.exp(g)
        g_hi, g_lo = _split_bf16(g * LOG2_E)
        es.append(jnp.exp2(_dot(sums, jnp.concatenate([g_hi, g_lo], axis=0))))
        qs.append(q)
        ks.append(k)

    q16 = [q.astype(BF16) for q in qs]
    k16 = [k.astype(BF16) for k in ks]
    raw = {(ck, h): _dot_nt(q16[ck][:, lanes[h]], k16[ck][:, lanes[h]]) for ck, h in pairs}
    a16 = {p: jnp.where(code == 0, raw[p].astype(BF16), jnp.zeros((c, c), BF16)) for p in pairs}
    for lvl in range(1, nl + 1):
        e16 = [e[lvl * c:(lvl + 1) * c].astype(BF16) for e in es]
        qm = [q16[ck] * e16[ck] for ck in range(len(chunks))]
        km = [k16[ck] * e16[ck] for ck in range(len(chunks))]
        raw = {(ck, h): _dot_nt(qm[ck][:, lanes[h]], km[ck][:, lanes[h]]) for ck, h in pairs}
        a16 = {p: jnp.where(code == lvl, raw[p].astype(BF16), a16[p]) for p in pairs}

    st = [st_ref[h] for h in range(REC_HEADS)]
    for ck, sl in enumerate(chunks):
        e_cum = es[ck][0:c]
        e_tail = es[ck][(nl + 1) * c:(nl + 2) * c]
        q_in = (qs[ck] * e_cum).astype(BF16)
        k_out = (ks[ck] * e_tail).astype(BF16)
        v16 = v_ref[sl, :].astype(BF16)
        for h in range(REC_HEADS):
            ls = lanes[h]
            o = _dot(a16[ck, h], v16[:, ls]) + _dot_nt(q_in[:, ls], st[h].astype(BF16))
            st[h] = st[h] * e_cum[c - 1:c, ls] + _dot_tn(v16[:, ls], k_out[:, ls])
            y_ref[sl, ls] = _rec_finish(o, og_ref[sl, ls], gain).astype(y_ref.dtype)
    for h in range(REC_HEADS):
        st_ref[h] = st[h]


def _rec_call(kind, proj, col_blocks, params):
    nt = SEQ // REC_TB
    in_specs, args = [], []
    for cb, width in col_blocks:
        in_specs.append(pl.BlockSpec((REC_TB, width * LANE),
                                     lambda b, t, cb=cb, width=width: (b * nt + t, cb // width)))
        args.append(proj)
    for p in params:
        in_specs.append(pl.BlockSpec(p.shape, lambda b, t: (0, 0)))
        args.append(p)
    wide = REC_HEADS * LANE
    return pl.pallas_call(
        functools.partial(_rec_kernel, kind),
        grid=(BATCH, nt),
        in_specs=in_specs,
        out_specs=pl.BlockSpec((REC_TB, wide), lambda b, t: (b * nt + t, 0)),
        out_shape=jax.ShapeDtypeStruct((TOKENS, wide), BF16),
        scratch_shapes=[pltpu.VMEM((REC_HEADS, LANE, LANE), F32)],
        compiler_params=_cparams("parallel", "arbitrary"),
        name=kind + "_recurrence",
    )(*args)


def _mla_proj_kernel(qc_ref, kvc_ref, kpe_ref, cs_ref, qn_ref, kvn_ref, wq_ref, wk_ref, wvt_ref,
                     q_ref, kn_ref, vt_ref, kr_ref):
    scale = (MLA_NOPE + MLA_ROPE) ** -0.5
    cs = cs_ref[...]
    qn = _rms(qc_ref[...], qn_ref[...]).astype(BF16)
    q = _dot(qn, wq_ref[...])
    for h in range(MLA_HEADS):
        lo = h * 2 * LANE
        q_ref[:, lo:lo + LANE] = (q[:, lo:lo + LANE] * scale).astype(q_ref.dtype)
        q_ref[:, lo + LANE:lo + 2 * LANE] = (q[:, lo + LANE:lo + 2 * LANE] * (cs * scale)).astype(q_ref.dtype)
    kvn = _rms(kvc_ref[...], kvn_ref[...]).astype(BF16)
    kn_ref[...] = _dot(kvn, wk_ref[...]).astype(kn_ref.dtype)
    vt = _dot_nt(wvt_ref[...], kvn)
    ones = jnp.ones((VT_ROWS - MLA_DV, vt.shape[1]), vt_ref.dtype)
    for h in range(MLA_HEADS):
        vt_ref[h * VT_ROWS:h * VT_ROWS + MLA_DV, :] = vt[h * MLA_DV:(h + 1) * MLA_DV].astype(vt_ref.dtype)
        vt_ref[h * VT_ROWS + MLA_DV:(h + 1) * VT_ROWS, :] = ones
    t = kpe_ref[...] * cs
    kr_ref[...] = (t + pltpu.roll(t, LANE // 2, 1)).astype(kr_ref.dtype)


def _mla_proj(proj, cs, qn_gain, kvn_gain, wq, wk, wvt, layer):
    tm = MLA_TM
    wide = 4 * LANE
    nq = MLA_HEADS * 2 * LANE
    nk = MLA_HEADS * MLA_NOPE
    nvt = MLA_HEADS * VT_ROWS
    return pl.pallas_call(
        _mla_proj_kernel,
        grid=(TOKENS // tm,),
        in_specs=[
            pl.BlockSpec((tm, wide), lambda i: (i, CB_QC // 4)),
            pl.BlockSpec((tm, wide), lambda i: (i, CB_KVC // 4)),
            pl.BlockSpec((tm, LANE), lambda i: (i, CB_KPE)),
            pl.BlockSpec((tm, LANE), lambda i: (i, 0)),
            pl.BlockSpec((1, MLA_Q_RANK), lambda i: (0, 0)),
            pl.BlockSpec((1, MLA_KV_RANK), lambda i: (0, 0)),
            _layer_spec(wq.shape[1:], lambda i: (0, 0), layer),
            _layer_spec(wk.shape[1:], lambda i: (0, 0), layer),
            _layer_spec(wvt.shape[1:], lambda i: (0, 0), layer),
        ],
        out_specs=[
            pl.BlockSpec((tm, nq), lambda i: (i, 0)),
            pl.BlockSpec((tm, nk), lambda i: (i, 0)),
            pl.BlockSpec((nvt, tm), lambda i: (0, i)),
            pl.BlockSpec((tm, LANE), lambda i: (i, 0)),
        ],
        out_shape=[
            jax.ShapeDtypeStruct((TOKENS, nq), BF16),
            jax.ShapeDtypeStruct((TOKENS, nk), BF16),
            jax.ShapeDtypeStruct((nvt, TOKENS), BF16),
            jax.ShapeDtypeStruct((TOKENS, LANE), BF16),
        ],
        compiler_params=_cparams("parallel"),
        name="mla_proj",
    )(proj, proj, proj, cs, qn_gain, kvn_gain, wq, wk, wvt)


def _attn_kernel(q_ref, kn_ref, kr_ref, vt_ref, o_ref):
    tq = ATT_TQ
    qi = pl.program_id(2)
    qs = [q_ref[:, h * 2 * LANE:(h + 1) * 2 * LANE] for h in range(ATT_HEADS)]

    def step(j, state, masked):
        start = pl.multiple_of(j * tq, tq)
        kr = kr_ref[pl.ds(start, tq), :]
        s_blocks = []
        for h in range(ATT_HEADS):
            k = jnp.concatenate([kn_ref[pl.ds(start, tq), h * LANE:(h + 1) * LANE], kr], axis=1)
            s_blocks.append(_dot_nt(k, qs[h]))
        out = []
        for h in range(ATT_HEADS):
            m, acc = state[h]
            s = s_blocks[h]
            if masked:
                ki = lax.broadcasted_iota(jnp.int32, s.shape, 0)
                qj = lax.broadcasted_iota(jnp.int32, s.shape, 1)
                s = jnp.where(ki <= qj, s, -jnp.inf)
            m_new = jnp.maximum(m, jnp.max(s, axis=0, keepdims=True))
            alpha = jnp.exp(m - m_new)
            p = jnp.exp((s - m_new).astype(BF16))
            acc = alpha * acc + _dot(vt_ref[h * VT_ROWS:(h + 1) * VT_ROWS, pl.ds(start, tq)], p)
            out.append((m_new, acc))
        return tuple(out)

    init = tuple((jnp.full((1, tq), -jnp.inf, F32), jnp.zeros((VT_ROWS, tq), F32)) for _ in range(ATT_HEADS))
    state = lax.fori_loop(0, qi, functools.partial(step, masked=False), init)
    final = step(qi, state, masked=True)
    for h in range(ATT_HEADS):
        _, acc = final[h]
        o = acc[:MLA_DV] / acc[MLA_DV:MLA_DV + 1]
        o_ref[:, h * LANE:(h + 1) * LANE] = o.T.astype(o_ref.dtype)


def _attention(q, kn, kr, vt):
    tq = ATT_TQ
    nq = SEQ // tq
    hw = ATT_HEADS * LANE
    return pl.pallas_call(
        _attn_kernel,
        grid=(BATCH, MLA_HEADS // ATT_HEADS, nq),
        in_specs=[
            pl.BlockSpec((tq, 2 * hw), lambda b, h, i: (b * nq + i, h)),
            pl.BlockSpec((SEQ, hw), lambda b, h, i: (b, h)),
            pl.BlockSpec((SEQ, LANE), lambda b, h, i: (b, 0)),
            pl.BlockSpec((ATT_HEADS * VT_ROWS, SEQ), lambda b, h, i: (h, b)),
        ],
        out_specs=pl.BlockSpec((tq, hw), lambda b, h, i: (b * nq + i, h)),
        out_shape=jax.ShapeDtypeStruct((TOKENS, MLA_HEADS * MLA_DV), F32),
        compiler_params=_cparams("parallel", "parallel", "arbitrary"),
        name="mla_attention",
    )(q, kn, kr, vt)


def _out_proj_kernel(yg_ref, yh_ref, om_ref, x_ref, mg_ref, w_ref, pg_ref, o_ref, w16_ref):
    @pl.when(pl.program_id(0) == 0)
    def _():
        w16_ref[...] = w_ref[...].astype(BF16)

    ym = _rms(om_ref[...], mg_ref[...]).astype(BF16)
    mix = jnp.concatenate([yg_ref[...], yh_ref[...], ym], axis=1)
    m = _dot(mix, w16_ref[...])
    o_ref[...] = x_ref[...] + _rms(m, pg_ref[...])


def _out_proj(y_gla, y_hg, o_mla, x, mla_gain, w_out, layer, post_gain):
    tm = OUT_TM
    d = D_MODEL
    return pl.pallas_call(
        _out_proj_kernel,
        grid=(TOKENS // tm,),
        in_specs=[
            pl.BlockSpec((tm, y_gla.shape[1]), lambda i: (i, 0)),
            pl.BlockSpec((tm, y_hg.shape[1]), lambda i: (i, 0)),
            pl.BlockSpec((tm, o_mla.shape[1]), lambda i: (i, 0)),
            pl.BlockSpec((tm, d), lambda i: (i, 0)),
            pl.BlockSpec((1, o_mla.shape[1]), lambda i: (0, 0)),
            _layer_spec(w_out.shape[1:], lambda i: (0, 0), layer, pipeline_mode=pl.Buffered(1)),
            pl.BlockSpec((1, d), lambda i: (0, 0)),
        ],
        out_specs=pl.BlockSpec((tm, d), lambda i: (i, 0)),
        out_shape=jax.ShapeDtypeStruct((TOKENS, d), F32),
        scratch_shapes=[pltpu.VMEM(w_out.shape[1:], BF16)],
        compiler_params=_cparams("arbitrary"),
        name="out_proj",
    )(y_gla, y_hg, o_mla, x, mla_gain, w_out, post_gain)


def _ffn_kernel(x_ref, ug_ref, wg_ref, wu_ref, wd_ref, pg_ref, o_ref, u_ref):
    j = pl.program_id(1)

    @pl.when(j == 0)
    def _():
        u_ref[...] = _rms(x_ref[...], ug_ref[...]).astype(BF16)
        o_ref[...] = jnp.zeros_like(o_ref)

    u = u_ref[...]
    gate = _dot(u, wg_ref[...].astype(BF16))
    up = _dot(u, wu_ref[...].astype(BF16))
    hid = (gate * _sigmoid(gate) * up).astype(BF16)
    o_ref[...] += _dot(hid, wd_ref[...].astype(BF16))

    @pl.when(j == pl.num_programs(1) - 1)
    def _():
        o_ref[...] = x_ref[...] + _rms(o_ref[...], pg_ref[...])


def _ffn(x, pre_gain, wg, wu, wd, layer, post_gain):
    tm, tf = FFN_TM, FFN_TF
    d = D_MODEL
    once = pl.Buffered(1)
    return pl.pallas_call(
        _ffn_kernel,
        grid=(TOKENS // tm, D_FF // tf),
        in_specs=[
            pl.BlockSpec((tm, d), lambda i, j: (i, 0), pipeline_mode=once),
            pl.BlockSpec((1, d), lambda i, j: (0, 0)),
            _layer_spec((d, tf), lambda i, j: (0, j), layer),
            _layer_spec((d, tf), lambda i, j: (0, j), layer),
            _layer_spec((tf, d), lambda i, j: (j, 0), layer),
            pl.BlockSpec((1, d), lambda i, j: (0, 0)),
        ],
        out_specs=pl.BlockSpec((tm, d), lambda i, j: (i, 0)),
        out_shape=jax.ShapeDtypeStruct((TOKENS, d), F32),
        scratch_shapes=[pltpu.VMEM((tm, d), BF16)],
        compiler_params=_cparams("parallel", "arbitrary"),
        name="ffn",
    )(x, pre_gain, wg, wu, wd, post_gain)


def _rot_cols(w):
    half = MLA_ROPE // 2
    return jnp.concatenate([-w[..., half:], w[..., :half]], axis=-1)


def _pad_heads(w, heads, width):
    lead = w.shape[:-1]
    w = w.reshape(lead + (heads, width))
    w = jnp.pad(w, [(0, 0)] * len(lead) + [(0, 0), (0, LANE - width)])
    return w.reshape(lead + (heads * LANE,))


W_IN_SIZES = (GLA_HEADS * GLA_DK, GLA_HEADS * GLA_DK, GLA_HEADS * GLA_DV, GLA_GATE_RANK,
              GLA_HEADS * GLA_DV, HGRN_HEADS * HGRN_DK, HGRN_HEADS * HGRN_DK, HGRN_HEADS * HGRN_DV,
              HGRN_HEADS * HGRN_DV, MLA_Q_RANK, MLA_KV_RANK, MLA_ROPE)
W_IN_COLS = 256


def _layout_w_in_kernel(wt_ref, o_ref):
    wt = wt_ref[...]
    cols = wt.shape[1]
    off = [int(c) for c in np.cumsum((0,) + W_IN_SIZES)]
    gq, gk, gv, g_low, g_out, hq, _, _, _, qc, _, kpe = off[:12]
    half = MLA_ROPE // 2
    pad_dk = jnp.zeros((LANE - GLA_DK, cols), F32)
    rows = []
    for base in (gq, gk):
        for h in range(GLA_HEADS):
            rows += [wt[base + h * GLA_DK:base + (h + 1) * GLA_DK], pad_dk]
    rows += [wt[gv:g_low], wt[g_out:hq], wt[hq:qc], wt[qc:kpe]]
    pe = wt[kpe:kpe + MLA_ROPE]
    rows += [pe, -pe[half:], pe[:half]]
    rows += [wt[g_low:g_out], jnp.zeros((LANE - GLA_GATE_RANK, cols), F32)]
    o_ref[...] = jnp.concatenate(rows, axis=0).T.astype(o_ref.dtype)


def _layout_w_in(w_in):
    depth, d, n = w_in.shape
    w_in_t = jnp.swapaxes(w_in, 1, 2)
    return pl.pallas_call(
        _layout_w_in_kernel,
        grid=(depth, d // W_IN_COLS),
        in_specs=[pl.BlockSpec((None, n, W_IN_COLS), lambda l, i: (l, 0, i))],
        out_specs=pl.BlockSpec((None, W_IN_COLS, PROJ_COLS), lambda l, i: (l, i, 0)),
        out_shape=jax.ShapeDtypeStruct((depth, d, PROJ_COLS), BF16),
        compiler_params=_cparams("parallel", "parallel"),
        name="layout_w_in",
    )(w_in_t)


def _layout_wq(wq_b):
    lead = wq_b.shape[:-1]
    w = wq_b.reshape(lead + (MLA_HEADS, MLA_NOPE + MLA_ROPE))
    nope, pe = w[..., :MLA_NOPE], w[..., MLA_NOPE:]
    w = jnp.concatenate([nope, pe, _rot_cols(pe)], axis=-1)
    return w.reshape(lead + (MLA_HEADS * 2 * LANE,)).astype(BF16)


def _layout_wkv(wkv_b):
    lead = wkv_b.shape[:-1]
    w = wkv_b.reshape(lead + (MLA_HEADS, MLA_NOPE + MLA_DV))
    k = w[..., :MLA_NOPE].reshape(lead + (MLA_HEADS * MLA_NOPE,))
    v = w[..., MLA_NOPE:].reshape(lead + (MLA_HEADS * MLA_DV,))
    return k.astype(BF16), jnp.swapaxes(v, -1, -2).astype(BF16)


def kernel(x, positions, attn_pre_norm, w_in, gla_gate_w2, gla_gate_b, gla_out_norm, hgrn_lb_logits,
           hgrn_out_norm, mla_q_norm, mla_wq_b, mla_kv_norm, mla_wkv_b, mla_out_norm, w_out,
           attn_post_norm, ffn_pre_norm, w_gate, w_up, w_down, ffn_post_norm):
    depth = w_in.shape[0]
    inv_freq = ROPE_THETA ** (-jnp.arange(0, MLA_ROPE, 2, dtype=F32) / MLA_ROPE)
    ang = positions.astype(F32).reshape(TOKENS, 1) * inv_freq
    cs = jnp.concatenate([jnp.cos(ang), jnp.cos(ang), jnp.sin(ang), jnp.sin(ang)], axis=-1)
    csum = jnp.cumsum(jax.nn.softmax(hgrn_lb_logits.astype(F32), axis=0), axis=0)
    lb = csum - csum[0]
    log_lb = jnp.log(lb)
    log_1m_lb = jnp.log1p(-lb)

    w_in_l = _layout_w_in(w_in)
    wq_l = _layout_wq(mla_wq_b)
    wk_l, wvt_l = _layout_wkv(mla_wkv_b)
    w2 = jnp.pad(_pad_heads(gla_gate_w2, GLA_HEADS, GLA_DK), [(0, 0), (0, LANE - GLA_GATE_RANK), (0, 0)])
    b2 = _pad_heads(gla_gate_b, GLA_HEADS, GLA_DK)
    sums = jnp.asarray(_decay_sum_matrix(), dtype=BF16)

    def row(p, l):
        return p[l].reshape(1, -1)

    xf = x.reshape(TOKENS, D_MODEL)
    for l in range(depth):
        proj = _norm_matmul(xf, row(attn_pre_norm, l), w_in_l, l, F32)
        y_gla = _rec_call(
            "gla", proj,
            [(CB_GQ, 4), (CB_GK, 4), (CB_GV, 4), (CB_GLOW, 1), (CB_GOUT, 4)],
            [w2[l], row(b2, l), row(gla_out_norm, l), sums])
        y_hg = _rec_call(
            "hgrn", proj,
            [(CB_HQ, 4), (CB_HF, 4), (CB_HI, 4), (CB_HOUT, 4)],
            [row(log_lb, l), row(log_1m_lb, l), row(hgrn_out_norm, l), sums])
        q, kn, vt, kr = _mla_proj(proj, cs, row(mla_q_norm, l), row(mla_kv_norm, l), wq_l, wk_l, wvt_l, l)
        o_mla = _attention(q, kn, kr, vt)
        xf = _out_proj(y_gla, y_hg, o_mla, xf, row(mla_out_norm, l), w_out, l, row(attn_post_norm, l))
        xf = _ffn(xf, row(ffn_pre_norm, l), w_gate, w_up, w_down, l, row(ffn_post_norm, l))
    return xf.reshape(x.shape)
```

```python
import functools

import numpy as np
import jax
import jax.numpy as jnp
from jax import lax
from jax.experimental import pallas as pl
from jax.experimental.pallas import tpu as pltpu

F32 = jnp.float32
BF16 = jnp.bfloat16

D_MODEL = 2048
BATCH = 2
SEQ = 4096
DEPTH = 4
TOKENS = BATCH * SEQ
GLA_HEADS = 4
GLA_DK = 64
GLA_DV = 128
GLA_GATE_RANK = 16
GLA_GATE_NORMALIZER = 16.0
HGRN_HEADS = 4
HGRN_DK = 128
HGRN_DV = 128
MLA_HEADS = 8
MLA_Q_RANK = 512
MLA_KV_RANK = 512
MLA_NOPE = 128
MLA_ROPE = 64
MLA_DV = 128
ROPE_THETA = 10000.0
D_FF = -(-8 * D_MODEL // (3 * 256)) * 256
EPS = 1e-6
LOG2_E = float(np.log2(np.e))

LANE = 128
VMEM_LIMIT = 60 * 1024 * 1024

CB_GQ, CB_GK, CB_GV, CB_GOUT = 0, 4, 8, 12
CB_HQ, CB_HF, CB_HI, CB_HOUT = 16, 20, 24, 28
CB_QC, CB_KVC = 32, 36
CB_KPE, CB_GLOW = 40, 41
PROJ_COLS = 42 * LANE

PROJ_TM, PROJ_TN = 1024, 1792
REC_HEADS = 4
REC_CHUNK = 128
REC_TB = 512
REC_LEVELS = tuple(2 ** p for p in range(1, 8))
MLA_TM = 512
ATT_TQ = 512
ATT_HEADS = 4
VT_ROWS = MLA_DV + 16
OUT_TM = 512
FFN_TM, FFN_TF = 1024, 512


def _cparams(*sem):
    return pltpu.CompilerParams(dimension_semantics=sem, vmem_limit_bytes=VMEM_LIMIT)


def _dot(a, b):
    return jnp.dot(a, b, preferred_element_type=F32)


def _dot_nt(a, b):
    return lax.dot_general(a, b, (((1,), (1,)), ((), ())), preferred_element_type=F32)


def _dot_tn(a, b):
    return lax.dot_general(a, b, (((0,), (0,)), ((), ())), preferred_element_type=F32)


def _split_bf16(x):
    hi = x.astype(BF16)
    lo = (x - hi.astype(F32)).astype(BF16)
    return hi, lo


def _rms(x, gain):
    ms = jnp.mean(x * x, axis=-1, keepdims=True)
    return x * lax.rsqrt(ms + EPS) * gain


def _sigmoid(x):
    return 1.0 / (1.0 + jnp.exp(-x))


def _softplus_neg_abs(x):
    return jnp.log(1.0 + jnp.exp(-jnp.abs(x)))


def _log_sigmoid(x):
    return jnp.minimum(x, 0.0) - _softplus_neg_abs(x)


def _layer_spec(block, index_map, layer, **kwargs):
    return pl.BlockSpec((None,) + block, lambda *g: (layer,) + index_map(*g), **kwargs)


def _norm_matmul_kernel(x_ref, g_ref, w_ref, o_ref, h_ref):
    @pl.when(pl.program_id(1) == 0)
    def _():
        h_ref[...] = _rms(x_ref[...], g_ref[...]).astype(BF16)

    o_ref[...] = _dot(h_ref[...], w_ref[...]).astype(o_ref.dtype)


def _norm_matmul(x, gain, w, layer, out_dtype):
    m, d = x.shape
    n = w.shape[-1]
    return pl.pallas_call(
        _norm_matmul_kernel,
        grid=(m // PROJ_TM, n // PROJ_TN),
        in_specs=[
            pl.BlockSpec((PROJ_TM, d), lambda i, j: (i, 0)),
            pl.BlockSpec((1, d), lambda i, j: (0, 0)),
            _layer_spec((d, PROJ_TN), lambda i, j: (0, j), layer),
        ],
        out_specs=pl.BlockSpec((PROJ_TM, PROJ_TN), lambda i, j: (i, j)),
        out_shape=jax.ShapeDtypeStruct((m, n), out_dtype),
        scratch_shapes=[pltpu.VMEM((PROJ_TM, d), BF16)],
        compiler_params=_cparams("parallel", "arbitrary"),
        name="norm_proj",
    )(x, gain, w)


def _decay_sum_matrix():
    c = REC_CHUNK
    t = np.arange(c)
    blocks = [(t[None, :] <= t[:, None])]
    for m in REC_LEVELS:
        half = m // 2
        pos = t % m
        mid = t - pos + half
        upper = (pos >= half)[:, None] & (t[None, :] >= mid[:, None]) & (t[None, :] <= t[:, None])
        lower = (pos < half)[:, None] & (t[None, :] > t[:, None]) & (t[None, :] < mid[:, None])
        blocks.append(upper | lower)
    blocks.append(t[None, :] > t[:, None])
    sums = np.concatenate(blocks, axis=0).astype(np.float32)
    return np.concatenate([sums, sums], axis=1)


def _pair_level_codes():
    c = REC_CHUNK
    ri = lax.broadcasted_iota(jnp.int32, (c, c), 0)
    ci = lax.broadcasted_iota(jnp.int32, (c, c), 1)
    lvl = 32 - lax.clz(ri ^ ci)
    code = jnp.where(ri > ci, lvl, jnp.where(ri == ci, 0, -1))
    return code.astype(F32).astype(BF16)


def _rec_finish(o, og, gain):
    return _rms(o, gain) * (og * _sigmoid(og))


def _rec_kernel(kind, *refs):
    if kind == "gla":
        q_ref, k_ref, v_ref, gl_ref, og_ref, w2_ref, b2_ref, gain_ref, sums_ref, y_ref, st_ref = refs
    else:
        q_ref, f_ref, v_ref, og_ref, llb_ref, l1m_ref, gain_ref, sums_ref, y_ref, st_ref = refs

    @pl.when(pl.program_id(1) == 0)
    def _():
        st_ref[...] = jnp.zeros_like(st_ref)

    c = REC_CHUNK
    nl = len(REC_LEVELS)
    sums = sums_ref[...]
    gain = gain_ref[...]
    code = _pair_level_codes()
    chunks = [slice(ck * c, (ck + 1) * c) for ck in range(REC_TB // c)]
    lanes = [slice(h * LANE, (h + 1) * LANE) for h in range(REC_HEADS)]
    pairs = [(ck, h) for ck in range(len(chunks)) for h in range(REC_HEADS)]
    if kind == "gla":
        w2_hi, w2_lo = _split_bf16(w2_ref[...])

    qs, ks, es = [], [], []
    for sl in chunks:
        if kind == "gla":
            gl_hi, gl_lo = _split_bf16(gl_ref[sl, :])
            z = _dot(gl_hi, w2_hi) + _dot(gl_lo, w2_hi) + _dot(gl_hi, w2_lo) + b2_ref[...]
            g = _log_sigmoid(z) * (1.0 / GLA_GATE_NORMALIZER)
            q = q_ref[sl, :] * (GLA_DK ** -0.5)
            k = k_ref[sl, :]
        else:
            hq = q_ref[sl, :]
            q = hq * _sigmoid(hq)
            lo = llb_ref[...]
            hi = l1m_ref[...] + _log_sigmoid(f_ref[sl, :])
            g = jnp.maximum(lo, hi) + _softplus_neg_abs(lo - hi)
            k = 1.0 - jnp.exp(g)
        g_hi, g_lo = _split_bf16(g * LOG2_E)
        es.append(jnp.exp2(_dot(sums, jnp.concatenate([g_hi, g_lo], axis=0))))
        qs.append(q)
        ks.append(k)

    q16 = [q.astype(BF16) for q in qs]
    k16 = [k.astype(BF16) for k in ks]
    raw = {(ck, h): _dot_nt(q16[ck][:, lanes[h]], k16[ck][:, lanes[h]]) for ck, h in pairs}
    a16 = {p: jnp.where(code == 0, raw[p].astype(BF16), jnp.zeros((c, c), BF16)) for p in pairs}
    for lvl in range(1, nl + 1):
        e16 = [e[lvl * c:(lvl + 1) * c].astype(BF16) for e in es]
        qm = [q16[ck] * e16[ck] for ck in range(len(chunks))]
        km = [k16[ck] * e16[ck] for ck in range(len(chunks))]
        raw = {(ck, h): _dot_nt(qm[ck][:, lanes[h]], km[ck][:, lanes[h]]) for ck, h in pairs}
        a16 = {p: jnp.where(code == lvl, raw[p].astype(BF16), a16[p]) for p in pairs}

    st = [st_ref[h] for h in range(REC_HEADS)]
    for ck, sl in enumerate(chunks):
        e_cum = es[ck][0:c]
        e_tail = es[ck][(nl + 1) * c:(nl + 2) * c]
        q_in = (qs[ck] * e_cum).astype(BF16)
        k_out = (ks[ck] * e_tail).astype(BF16)
        v16 = v_ref[sl, :].astype(BF16)
        for h in range(REC_HEADS):
            ls = lanes[h]
            o = _dot(a16[ck, h], v16[:, ls]) + _dot_nt(q_in[:, ls], st[h].astype(BF16))
            st[h] = st[h] * e_cum[c - 1:c, ls] + _dot_tn(v16[:, ls], k_out[:, ls])
            y_ref[sl, ls] = _rec_finish(o, og_ref[sl, ls], gain).astype(y_ref.dtype)
    for h in range(REC_HEADS):
        st_ref[h] = st[h]


def _rec_call(kind, proj, col_blocks, params):
    nt = SEQ // REC_TB
    in_specs, args = [], []
    for cb, width in col_blocks:
        in_specs.append(pl.BlockSpec((REC_TB, width * LANE),
                                     lambda b, t, cb=cb, width=width: (b * nt + t, cb // width)))
        args.append(proj)
    for p in params:
        in_specs.append(pl.BlockSpec(p.shape, lambda b, t: (0, 0)))
        args.append(p)
    wide = REC_HEADS * LANE
    return pl.pallas_call(
        functools.partial(_rec_kernel, kind),
        grid=(BATCH, nt),
        in_specs=in_specs,
        out_specs=pl.BlockSpec((REC_TB, wide), lambda b, t: (b * nt + t, 0)),
        out_shape=jax.ShapeDtypeStruct((TOKENS, wide), BF16),
        scratch_shapes=[pltpu.VMEM((REC_HEADS, LANE, LANE), F32)],
        compiler_params=_cparams("parallel", "arbitrary"),
        name=kind + "_recurrence",
    )(*args)


def _mla_proj_kernel(qc_ref, kvc_ref, kpe_ref, cs_ref, qn_ref, kvn_ref, wq_ref, wk_ref, wvt_ref,
                     q_ref, kn_ref, vt_ref, kr_ref):
    scale = (MLA_NOPE + MLA_ROPE) ** -0.5
    cs = cs_ref[...]
    qn = _rms(qc_ref[...], qn_ref[...]).astype(BF16)
    q = _dot(qn, wq_ref[...])
    for h in range(MLA_HEADS):
        lo = h * 2 * LANE
        q_ref[:, lo:lo + LANE] = (q[:, lo:lo + LANE] * scale).astype(q_ref.dtype)
        q_ref[:, lo + LANE:lo + 2 * LANE] = (q[:, lo + LANE:lo + 2 * LANE] * (cs * scale)).astype(q_ref.dtype)
    kvn = _rms(kvc_ref[...], kvn_ref[...]).astype(BF16)
    kn_ref[...] = _dot(kvn, wk_ref[...]).astype(kn_ref.dtype)
    vt = _dot_nt(wvt_ref[...], kvn)
    ones = jnp.ones((VT_ROWS - MLA_DV, vt.shape[1]), vt_ref.dtype)
    for h in range(MLA_HEADS):
        vt_ref[h * VT_ROWS:h * VT_ROWS + MLA_DV, :] = vt[h * MLA_DV:(h + 1) * MLA_DV].astype(vt_ref.dtype)
        vt_ref[h * VT_ROWS + MLA_DV:(h + 1) * VT_ROWS, :] = ones
    t = kpe_ref[...] * cs
    kr_ref[...] = (t + pltpu.roll(t, LANE // 2, 1)).astype(kr_ref.dtype)


def _mla_proj(proj, cs, qn_gain, kvn_gain, wq, wk, wvt, layer):
    tm = MLA_TM
    wide = 4 * LANE
    nq = MLA_HEADS * 2 * LANE
    nk = MLA_HEADS * MLA_NOPE
    nvt = MLA_HEADS * VT_ROWS
    return pl.pallas_call(
        _mla_proj_kernel,
        grid=(TOKENS // tm,),
        in_specs=[
            pl.BlockSpec((tm, wide), lambda i: (i, CB_QC // 4)),
            pl.BlockSpec((tm, wide), lambda i: (i, CB_KVC // 4)),
            pl.BlockSpec((tm, LANE), lambda i: (i, CB_KPE)),
            pl.BlockSpec((tm, LANE), lambda i: (i, 0)),
            pl.BlockSpec((1, MLA_Q_RANK), lambda i: (0, 0)),
            pl.BlockSpec((1, MLA_KV_RANK), lambda i: (0, 0)),
            _layer_spec(wq.shape[1:], lambda i: (0, 0), layer),
            _layer_spec(wk.shape[1:], lambda i: (0, 0), layer),
            _layer_spec(wvt.shape[1:], lambda i: (0, 0), layer),
        ],
        out_specs=[
            pl.BlockSpec((tm, nq), lambda i: (i, 0)),
            pl.BlockSpec((tm, nk), lambda i: (i, 0)),
            pl.BlockSpec((nvt, tm), lambda i: (0, i)),
            pl.BlockSpec((tm, LANE), lambda i: (i, 0)),
        ],
        out_shape=[
            jax.ShapeDtypeStruct((TOKENS, nq), BF16),
            jax.ShapeDtypeStruct((TOKENS, nk), BF16),
            jax.ShapeDtypeStruct((nvt, TOKENS), BF16),
            jax.ShapeDtypeStruct((TOKENS, LANE), BF16),
        ],
        compiler_params=_cparams("parallel"),
        name="mla_proj",
    )(proj, proj, proj, cs, qn_gain, kvn_gain, wq, wk, wvt)


def _attn_kernel(q_ref, kn_ref, kr_ref, vt_ref, o_ref, s0_ref, s1_ref, m_ref, acc_ref):
    tq = ATT_TQ
    qi = pl.program_id(2)
    qs = [q_ref[:, h * 2 * LANE:(h + 1) * 2 * LANE] for h in range(ATT_HEADS)]

    def scores(j, s_ref):
        start = pl.multiple_of(j * tq, tq)
        kr = kr_ref[pl.ds(start, tq), :]
        for h in range(ATT_HEADS):
            k = jnp.concatenate([kn_ref[pl.ds(start, tq), h * LANE:(h + 1) * LANE], kr], axis=1)
            s_ref[h] = _dot_nt(k, qs[h])

    def absorb(j, s_ref, masked):
        start = pl.multiple_of(j * tq, tq)
        for h in range(ATT_HEADS):
            s = s_ref[h]
            if masked:
                ki = lax.broadcasted_iota(jnp.int32, s.shape, 0)
                qj = lax.broadcasted_iota(jnp.int32, s.shape, 1)
                s = jnp.where(ki <= qj, s, -jnp.inf)
            m = m_ref[h]
            m_new = jnp.maximum(m, jnp.max(s, axis=0, keepdims=True))
            alpha = jnp.exp(m - m_new)
            p = jnp.exp((s - m_new).astype(BF16))
            m_ref[h] = m_new
            acc_ref[h] = alpha * acc_ref[h] + _dot(vt_ref[h * VT_ROWS:(h + 1) * VT_ROWS, pl.ds(start, tq)], p)

    m_ref[...] = jnp.full(m_ref.shape, -jnp.inf, F32)
    acc_ref[...] = jnp.zeros(acc_ref.shape, F32)
    scores(0, s0_ref)

    def pair(t, carry):
        j = 2 * t
        scores(j + 1, s1_ref)
        absorb(j, s0_ref, masked=False)
        scores(j + 2, s0_ref)
        absorb(j + 1, s1_ref, masked=False)
        return carry

    lax.fori_loop(0, qi // 2, pair, 0)

    @pl.when(qi % 2 == 1)
    def _():
        scores(qi, s1_ref)
        absorb(qi - 1, s0_ref, masked=False)
        absorb(qi, s1_ref, masked=True)

    @pl.when(qi % 2 == 0)
    def _():
        absorb(qi, s0_ref, masked=True)

    for h in range(ATT_HEADS):
        acc = acc_ref[h]
        o = acc[:MLA_DV] / acc[MLA_DV:MLA_DV + 1]
        o_ref[:, h * LANE:(h + 1) * LANE] = o.T.astype(o_ref.dtype)


def _attention(q, kn, kr, vt):
    tq = ATT_TQ
    nq = SEQ // tq
    hw = ATT_HEADS * LANE
    return pl.pallas_call(
        _attn_kernel,
        grid=(BATCH, MLA_HEADS // ATT_HEADS, nq),
        in_specs=[
            pl.BlockSpec((tq, 2 * hw), lambda b, h, i: (b * nq + i, h)),
            pl.BlockSpec((SEQ, hw), lambda b, h, i: (b, h)),
            pl.BlockSpec((SEQ, LANE), lambda b, h, i: (b, 0)),
            pl.BlockSpec((ATT_HEADS * VT_ROWS, SEQ), lambda b, h, i: (h, b)),
        ],
        out_specs=pl.BlockSpec((tq, hw), lambda b, h, i: (b * nq + i, h)),
        out_shape=jax.ShapeDtypeStruct((TOKENS, MLA_HEADS * MLA_DV), F32),
        scratch_shapes=[pltpu.VMEM((ATT_HEADS, tq, tq), F32), pltpu.VMEM((ATT_HEADS, tq, tq), F32),
                        pltpu.VMEM((ATT_HEADS, 1, tq), F32), pltpu.VMEM((ATT_HEADS, VT_ROWS, tq), F32)],
        compiler_params=_cparams("parallel", "parallel", "arbitrary"),
        name="mla_attention",
    )(q, kn, kr, vt)


def _out_proj_kernel(yg_ref, yh_ref, om_ref, x_ref, mg_ref, w_ref, pg_ref, o_ref, w16_ref):
    @pl.when(pl.program_id(0) == 0)
    def _():
        w16_ref[...] = w_ref[...].astype(BF16)

    ym = _rms(om_ref[...], mg_ref[...]).astype(BF16)
    mix = jnp.concatenate([yg_ref[...], yh_ref[...], ym], axis=1)
    m = _dot(mix, w16_ref[...])
    o_ref[...] = x_ref[...] + _rms(m, pg_ref[...])


def _out_proj(y_gla, y_hg, o_mla, x, mla_gain, w_out, layer, post_gain):
    tm = OUT_TM
    d = D_MODEL
    return pl.pallas_call(
        _out_proj_kernel,
        grid=(TOKENS // tm,),
        in_specs=[
            pl.BlockSpec((tm, y_gla.shape[1]), lambda i: (i, 0)),
            pl.BlockSpec((tm, y_hg.shape[1]), lambda i: (i, 0)),
            pl.BlockSpec((tm, o_mla.shape[1]), lambda i: (i, 0)),
            pl.BlockSpec((tm, d), lambda i: (i, 0)),
            pl.BlockSpec((1, o_mla.shape[1]), lambda i: (0, 0)),
            _layer_spec(w_out.shape[1:], lambda i: (0, 0), layer, pipeline_mode=pl.Buffered(1)),
            pl.BlockSpec((1, d), lambda i: (0, 0)),
        ],
        out_specs=pl.BlockSpec((tm, d), lambda i: (i, 0)),
        out_shape=jax.ShapeDtypeStruct((TOKENS, d), F32),
        scratch_shapes=[pltpu.VMEM(w_out.shape[1:], BF16)],
        compiler_params=_cparams("arbitrary"),
        name="out_proj",
    )(y_gla, y_hg, o_mla, x, mla_gain, w_out, post_gain)


def _ffn_kernel(x_ref, ug_ref, wg_ref, wu_ref, wd_ref, pg_ref, o_ref, u_ref):
    j = pl.program_id(1)

    @pl.when(j == 0)
    def _():
        u_ref[...] = _rms(x_ref[...], ug_ref[...]).astype(BF16)
        o_ref[...] = jnp.zeros_like(o_ref)

    u = u_ref[...]
    gate = _dot(u, wg_ref[...].astype(BF16))
    up = _dot(u, wu_ref[...].astype(BF16))
    hid = (gate * _sigmoid(gate) * up).astype(BF16)
    o_ref[...] += _dot(hid, wd_ref[...].astype(BF16))

    @pl.when(j == pl.num_programs(1) - 1)
    def _():
        o_ref[...] = x_ref[...] + _rms(o_ref[...], pg_ref[...])


def _ffn(x, pre_gain, wg, wu, wd, layer, post_gain):
    tm, tf = FFN_TM, FFN_TF
    d = D_MODEL
    once = pl.Buffered(1)
    return pl.pallas_call(
        _ffn_kernel,
        grid=(TOKENS // tm, D_FF // tf),
        in_specs=[
            pl.BlockSpec((tm, d), lambda i, j: (i, 0), pipeline_mode=once),
            pl.BlockSpec((1, d), lambda i, j: (0, 0)),
            _layer_spec((d, tf), lambda i, j: (0, j), layer),
            _layer_spec((d, tf), lambda i, j: (0, j), layer),
            _layer_spec((tf, d), lambda i, j: (j, 0), layer),
            pl.BlockSpec((1, d), lambda i, j: (0, 0)),
        ],
        out_specs=pl.BlockSpec((tm, d), lambda i, j: (i, 0), pipeline_mode=once),
        out_shape=jax.ShapeDtypeStruct((TOKENS, d), F32),
        scratch_shapes=[pltpu.VMEM((tm, d), BF16)],
        compiler_params=_cparams("parallel", "arbitrary"),
        name="ffn",
    )(x, pre_gain, wg, wu, wd, post_gain)


def _rot_cols(w):
    half = MLA_ROPE // 2
    return jnp.concatenate([-w[..., half:], w[..., :half]], axis=-1)


def _pad_heads(w, heads, width):
    lead = w.shape[:-1]
    w = w.reshape(lead + (heads, width))
    w = jnp.pad(w, [(0, 0)] * len(lead) + [(0, 0), (0, LANE - width)])
    return w.reshape(lead + (heads * LANE,))


W_IN_SIZES = (GLA_HEADS * GLA_DK, GLA_HEADS * GLA_DK, GLA_HEADS * GLA_DV, GLA_GATE_RANK,
              GLA_HEADS * GLA_DV, HGRN_HEADS * HGRN_DK, HGRN_HEADS * HGRN_DK, HGRN_HEADS * HGRN_DV,
              HGRN_HEADS * HGRN_DV, MLA_Q_RANK, MLA_KV_RANK, MLA_ROPE)
W_IN_COLS = 256


def _layout_w_in_kernel(wt_ref, o_ref):
    wt = wt_ref[...]
    cols = wt.shape[1]
    off = [int(c) for c in np.cumsum((0,) + W_IN_SIZES)]
    gq, gk, gv, g_low, g_out, hq, _, _, _, qc, _, kpe = off[:12]
    half = MLA_ROPE // 2
    pad_dk = jnp.zeros((LANE - GLA_DK, cols), F32)
    rows = []
    for base in (gq, gk):
        for h in range(GLA_HEADS):
            rows += [wt[base + h * GLA_DK:base + (h + 1) * GLA_DK], pad_dk]
    rows += [wt[gv:g_low], wt[g_out:hq], wt[hq:qc], wt[qc:kpe]]
    pe = wt[kpe:kpe + MLA_ROPE]
    rows += [pe, -pe[half:], pe[:half]]
    rows += [wt[g_low:g_out], jnp.zeros((LANE - GLA_GATE_RANK, cols), F32)]
    o_ref[...] = jnp.concatenate(rows, axis=0).T.astype(o_ref.dtype)


def _layout_w_in(w_in):
    depth, d, n = w_in.shape
    w_in_t = jnp.swapaxes(w_in, 1, 2)
    return pl.pallas_call(
        _layout_w_in_kernel,
        grid=(depth, d // W_IN_COLS),
        in_specs=[pl.BlockSpec((None, n, W_IN_COLS), lambda l, i: (l, 0, i))],
        out_specs=pl.BlockSpec((None, W_IN_COLS, PROJ_COLS), lambda l, i: (l, i, 0)),
        out_shape=jax.ShapeDtypeStruct((depth, d, PROJ_COLS), BF16),
        compiler_params=_cparams("parallel", "parallel"),
        name="layout_w_in",
    )(w_in_t)


def _layout_wq(wq_b):
    lead = wq_b.shape[:-1]
    w = wq_b.reshape(lead + (MLA_HEADS, MLA_NOPE + MLA_ROPE))
    nope, pe = w[..., :MLA_NOPE], w[..., MLA_NOPE:]
    w = jnp.concatenate([nope, pe, _rot_cols(pe)], axis=-1)
    return w.reshape(lead + (MLA_HEADS * 2 * LANE,)).astype(BF16)


def _layout_wkv(wkv_b):
    lead = wkv_b.shape[:-1]
    w = wkv_b.reshape(lead + (MLA_HEADS, MLA_NOPE + MLA_DV))
    k = w[..., :MLA_NOPE].reshape(lead + (MLA_HEADS * MLA_NOPE,))
    v = w[..., MLA_NOPE:].reshape(lead + (MLA_HEADS * MLA_DV,))
    return k.astype(BF16), jnp.swapaxes(v, -1, -2).astype(BF16)


def kernel(x, positions, attn_pre_norm, w_in, gla_gate_w2, gla_gate_b, gla_out_norm, hgrn_lb_logits,
           hgrn_out_norm, mla_q_norm, mla_wq_b, mla_kv_norm, mla_wkv_b, mla_out_norm, w_out,
           attn_post_norm, ffn_pre_norm, w_gate, w_up, w_down, ffn_post_norm):
    depth = w_in.shape[0]
    inv_freq = ROPE_THETA ** (-jnp.arange(0, MLA_ROPE, 2, dtype=F32) / MLA_ROPE)
    ang = positions.astype(F32).reshape(TOKENS, 1) * inv_freq
    cs = jnp.concatenate([jnp.cos(ang), jnp.cos(ang), jnp.sin(ang), jnp.sin(ang)], axis=-1)
    csum = jnp.cumsum(jax.nn.softmax(hgrn_lb_logits.astype(F32), axis=0), axis=0)
    lb = csum - csum[0]
    log_lb = jnp.log(lb)
    log_1m_lb = jnp.log1p(-lb)

    w_in_l = _layout_w_in(w_in)
    wq_l = _layout_wq(mla_wq_b)
    wk_l, wvt_l = _layout_wkv(mla_wkv_b)
    w2 = jnp.pad(_pad_heads(gla_gate_w2, GLA_HEADS, GLA_DK), [(0, 0), (0, LANE - GLA_GATE_RANK), (0, 0)])
    b2 = _pad_heads(gla_gate_b, GLA_HEADS, GLA_DK)
    sums = jnp.asarray(_decay_sum_matrix(), dtype=BF16)

    def row(p, l):
        return p[l].reshape(1, -1)

    xf = x.reshape(TOKENS, D_MODEL)
    for l in range(depth):
        proj = _norm_matmul(xf, row(attn_pre_norm, l), w_in_l, l, F32)
        y_gla = _rec_call(
            "gla", proj,
            [(CB_GQ, 4), (CB_GK, 4), (CB_GV, 4), (CB_GLOW, 1), (CB_GOUT, 4)],
            [w2[l], row(b2, l), row(gla_out_norm, l), sums])
        y_hg = _rec_call(
            "hgrn", proj,
            [(CB_HQ, 4), (CB_HF, 4), (CB_HI, 4), (CB_HOUT, 4)],
            [row(log_lb, l), row(log_1m_lb, l), row(hgrn_out_norm, l), sums])
        q, kn, vt, kr = _mla_proj(proj, cs, row(mla_q_norm, l), row(mla_kv_norm, l), wq_l, wk_l, wvt_l, l)
        o_mla = _attention(q, kn, kr, vt)
        xf = _out_proj(y_gla, y_hg, o_mla, xf, row(mla_out_norm, l), w_out, l, row(attn_post_norm, l))
        xf = _ffn(xf, row(ffn_pre_norm, l), w_gate, w_up, w_down, l, row(ffn_post_norm, l))
    return xf.reshape(x.shape)
```

```python
import functools

import numpy as np
import jax
import jax.numpy as jnp
from jax import lax
from jax.experimental import pallas as pl
from jax.experimental.pallas import tpu as pltpu

F32 = jnp.float32
BF16 = jnp.bfloat16

D_MODEL = 2048
BATCH = 2
SEQ = 4096
DEPTH = 4
TOKENS = BATCH * SEQ
GLA_HEADS = 4
GLA_DK = 64
GLA_DV = 128
GLA_GATE_RANK = 16
GLA_GATE_NORMALIZER = 16.0
HGRN_HEADS = 4
HGRN_DK = 128
HGRN_DV = 128
MLA_HEADS = 8
MLA_Q_RANK = 512
MLA_KV_RANK = 512
MLA_NOPE = 128
MLA_ROPE = 64
MLA_DV = 128
ROPE_THETA = 10000.0
D_FF = -(-8 * D_MODEL // (3 * 256)) * 256
EPS = 1e-6
LOG2_E = float(np.log2(np.e))

LANE = 128
VMEM_LIMIT = 60 * 1024 * 1024

CB_GQ, CB_GK, CB_GV, CB_GOUT = 0, 2, 4, 8
CB_HQ, CB_HF, CB_HI, CB_HOUT = 12, 16, 20, 24
CB_QC, CB_KVC = 28, 32
CB_KPE, CB_GLOW = 36, 37
PROJ_COLS = 40 * LANE

PROJ_TM, PROJ_TN = 1024, 1280
REC_HEADS = 4
REC_CHUNK = 128
REC_TB = 512
REC_LEVELS = tuple(2 ** p for p in range(1, 8))
MLA_TM = 512
ATT_TQ = 512
ATT_HEADS = 4
VT_ROWS = MLA_DV + 16
OUT_TM = 512
FFN_TM, FFN_TF = 1024, 512


def _cparams(*sem):
    return pltpu.CompilerParams(dimension_semantics=sem, vmem_limit_bytes=VMEM_LIMIT)


def _dot(a, b):
    return jnp.dot(a, b, preferred_element_type=F32)


def _dot_nt(a, b):
    return lax.dot_general(a, b, (((1,), (1,)), ((), ())), preferred_element_type=F32)


def _dot_tn(a, b):
    return lax.dot_general(a, b, (((0,), (0,)), ((), ())), preferred_element_type=F32)


def _split_bf16(x):
    hi = x.astype(BF16)
    lo = (x - hi.astype(F32)).astype(BF16)
    return hi, lo


def _rms(x, gain):
    ms = jnp.mean(x * x, axis=-1, keepdims=True)
    return x * lax.rsqrt(ms + EPS) * gain


def _sigmoid(x):
    return 1.0 / (1.0 + jnp.exp(-x))


def _softplus_neg_abs(x):
    return jnp.log(1.0 + jnp.exp(-jnp.abs(x)))


def _log_sigmoid(x):
    return jnp.minimum(x, 0.0) - _softplus_neg_abs(x)


def _layer_spec(block, index_map, layer, **kwargs):
    return pl.BlockSpec((None,) + block, lambda *g: (layer,) + index_map(*g), **kwargs)


def _norm_matmul_kernel(x_ref, g_ref, w_ref, o_ref, h_ref):
    @pl.when(pl.program_id(1) == 0)
    def _():
        h_ref[...] = _rms(x_ref[...], g_ref[...]).astype(BF16)

    o_ref[...] = _dot(h_ref[...], w_ref[...]).astype(o_ref.dtype)


def _norm_matmul(x, gain, w, layer, out_dtype):
    m, d = x.shape
    n = w.shape[-1]
    return pl.pallas_call(
        _norm_matmul_kernel,
        grid=(m // PROJ_TM, n // PROJ_TN),
        in_specs=[
            pl.BlockSpec((PROJ_TM, d), lambda i, j: (i, 0)),
            pl.BlockSpec((1, d), lambda i, j: (0, 0)),
            _layer_spec((d, PROJ_TN), lambda i, j: (0, j), layer),
        ],
        out_specs=pl.BlockSpec((PROJ_TM, PROJ_TN), lambda i, j: (i, j)),
        out_shape=jax.ShapeDtypeStruct((m, n), out_dtype),
        scratch_shapes=[pltpu.VMEM((PROJ_TM, d), BF16)],
        compiler_params=_cparams("parallel", "arbitrary"),
        name="norm_proj",
    )(x, gain, w)


def _decay_sum_matrix():
    c = REC_CHUNK
    t = np.arange(c)
    blocks = [(t[None, :] <= t[:, None])]
    for m in REC_LEVELS:
        half = m // 2
        pos = t % m
        mid = t - pos + half
        upper = (pos >= half)[:, None] & (t[None, :] >= mid[:, None]) & (t[None, :] <= t[:, None])
        lower = (pos < half)[:, None] & (t[None, :] > t[:, None]) & (t[None, :] < mid[:, None])
        blocks.append(upper | lower)
    blocks.append(t[None, :] > t[:, None])
    sums = np.concatenate(blocks, axis=0).astype(np.float32)
    return np.concatenate([sums, sums], axis=1)


def _pair_level_codes():
    c = REC_CHUNK
    ri = lax.broadcasted_iota(jnp.int32, (c, c), 0)
    ci = lax.broadcasted_iota(jnp.int32, (c, c), 1)
    lvl = 32 - lax.clz(ri ^ ci)
    code = jnp.where(ri > ci, lvl, jnp.where(ri == ci, 0, -1))
    return code.astype(F32).astype(BF16)


def _rec_finish(o, og, gain):
    return _rms(o, gain) * (og * _sigmoid(og))


def _rec_kernel(kind, *refs):
    if kind == "gla":
        q_ref, k_ref, v_ref, gl_ref, og_ref, w2_ref, b2_ref, gain_ref, sums_ref, y_ref, st_ref = refs
    else:
        q_ref, f_ref, v_ref, og_ref, llb_ref, l1m_ref, gain_ref, sums_ref, y_ref, st_ref = refs

    @pl.when(pl.program_id(1) == 0)
    def _():
        st_ref[...] = jnp.zeros_like(st_ref)

    c = REC_CHUNK
    nl = len(REC_LEVELS)
    sums = sums_ref[...]
    gain = gain_ref[...]
    code = _pair_level_codes()
    chunks = [slice(ck * c, (ck + 1) * c) for ck in range(REC_TB // c)]
    lanes = [slice(h * LANE, (h + 1) * LANE) for h in range(REC_HEADS)]
    pairs = [(ck, h) for ck in range(len(chunks)) for h in range(REC_HEADS)]
    pack = LANE // (GLA_DK if kind == "gla" else HGRN_DK)
    qk_lanes = [slice((h // pack) * LANE, (h // pack + 1) * LANE) for h in range(REC_HEADS)]
    if pack > 1:
        lane = lax.broadcasted_iota(jnp.int32, (1, LANE), 1) // (LANE // pack)
        kmask = [jnp.where(lane == h % pack, 1.0, 0.0).astype(BF16) for h in range(REC_HEADS)]
    else:
        kmask = [None] * REC_HEADS

    def head_k(k_all, h):
        blk = k_all[:, qk_lanes[h]]
        return blk if kmask[h] is None else blk * kmask[h]

    if kind == "gla":
        w2_hi, w2_lo = _split_bf16(w2_ref[...])

    qs, ks, es = [], [], []
    for sl in chunks:
        if kind == "gla":
            gl_hi, gl_lo = _split_bf16(gl_ref[sl, :])
            z = _dot(gl_hi, w2_hi) + _dot(gl_lo, w2_hi) + _dot(gl_hi, w2_lo) + b2_ref[...]
            g = _log_sigmoid(z) * (1.0 / GLA_GATE_NORMALIZER)
            q = q_ref[sl, :] * (GLA_DK ** -0.5)
            k = k_ref[sl, :]
        else:
            hq = q_ref[sl, :]
            q = hq * _sigmoid(hq)
            lo = llb_ref[...]
            hi = l1m_ref[...] + _log_sigmoid(f_ref[sl, :])
            g = jnp.maximum(lo, hi) + _softplus_neg_abs(lo - hi)
            k = 1.0 - jnp.exp(g)
        g_hi, g_lo = _split_bf16(g * LOG2_E)
        es.append(jnp.exp2(_dot(sums, jnp.concatenate([g_hi, g_lo], axis=0))))
        qs.append(q)
        ks.append(k)

    q16 = [q.astype(BF16) for q in qs]
    k16 = {(ck, h): head_k(ks[ck].astype(BF16), h) for ck, h in pairs}
    raw = {(ck, h): _dot_nt(q16[ck][:, qk_lanes[h]], k16[ck, h]) for ck, h in pairs}
    a16 = {p: jnp.where(code == 0, raw[p].astype(BF16), jnp.zeros((c, c), BF16)) for p in pairs}
    for lvl in range(1, nl + 1):
        e16 = [e[lvl * c:(lvl + 1) * c].astype(BF16) for e in es]
        qm = [q16[ck] * e16[ck] for ck in range(len(chunks))]
        raw = {(ck, h): _dot_nt(qm[ck][:, qk_lanes[h]], k16[ck, h] * e16[ck][:, qk_lanes[h]]) for ck, h in pairs}
        a16 = {p: jnp.where(code == lvl, raw[p].astype(BF16), a16[p]) for p in pairs}

    st = [st_ref[h] for h in range(REC_HEADS)]
    for ck, sl in enumerate(chunks):
        e_cum = es[ck][0:c]
        e_tail = es[ck][(nl + 1) * c:(nl + 2) * c]
        q_in = (qs[ck] * e_cum).astype(BF16)
        k_out = (ks[ck] * e_tail).astype(BF16)
        v16 = v_ref[sl, :].astype(BF16)
        for h in range(REC_HEADS):
            ls, kl = lanes[h], qk_lanes[h]
            o = _dot(a16[ck, h], v16[:, ls]) + _dot_nt(q_in[:, kl], st[h].astype(BF16))
            st[h] = st[h] * e_cum[c - 1:c, kl] + _dot_tn(v16[:, ls], head_k(k_out, h))
            y_ref[sl, ls] = _rec_finish(o, og_ref[sl, ls], gain).astype(y_ref.dtype)
    for h in range(REC_HEADS):
        st_ref[h] = st[h]


def _rec_call(kind, proj, col_blocks, params):
    nt = SEQ // REC_TB
    in_specs, args = [], []
    for cb, width in col_blocks:
        in_specs.append(pl.BlockSpec((REC_TB, width * LANE),
                                     lambda b, t, cb=cb, width=width: (b * nt + t, cb // width)))
        args.append(proj)
    for p in params:
        in_specs.append(pl.BlockSpec(p.shape, lambda b, t: (0, 0)))
        args.append(p)
    wide = REC_HEADS * LANE
    return pl.pallas_call(
        functools.partial(_rec_kernel, kind),
        grid=(BATCH, nt),
        in_specs=in_specs,
        out_specs=pl.BlockSpec((REC_TB, wide), lambda b, t: (b * nt + t, 0)),
        out_shape=jax.ShapeDtypeStruct((TOKENS, wide), BF16),
        scratch_shapes=[pltpu.VMEM((REC_HEADS, LANE, LANE), F32)],
        compiler_params=_cparams("parallel", "arbitrary"),
        name=kind + "_recurrence",
    )(*args)


def _mla_proj_kernel(qc_ref, kvc_ref, kpe_ref, cs_ref, qn_ref, kvn_ref, wq_ref, wk_ref, wvt_ref,
                     q_ref, kn_ref, vt_ref, kr_ref):
    scale = (MLA_NOPE + MLA_ROPE) ** -0.5
    cs = cs_ref[...]
    qn = _rms(qc_ref[...], qn_ref[...]).astype(BF16)
    q = _dot(qn, wq_ref[...])
    for h in range(MLA_HEADS):
        lo = h * 2 * LANE
        q_ref[:, lo:lo + LANE] = (q[:, lo:lo + LANE] * scale).astype(q_ref.dtype)
        q_ref[:, lo + LANE:lo + 2 * LANE] = (q[:, lo + LANE:lo + 2 * LANE] * (cs * scale)).astype(q_ref.dtype)
    kvn = _rms(kvc_ref[...], kvn_ref[...]).astype(BF16)
    kn_ref[...] = _dot(kvn, wk_ref[...]).astype(kn_ref.dtype)
    vt = _dot_nt(wvt_ref[...], kvn)
    ones = jnp.ones((VT_ROWS - MLA_DV, vt.shape[1]), vt_ref.dtype)
    for h in range(MLA_HEADS):
        vt_ref[h * VT_ROWS:h * VT_ROWS + MLA_DV, :] = vt[h * MLA_DV:(h + 1) * MLA_DV].astype(vt_ref.dtype)
        vt_ref[h * VT_ROWS + MLA_DV:(h + 1) * VT_ROWS, :] = ones
    t = kpe_ref[...] * cs
    kr_ref[...] = (t + pltpu.roll(t, LANE // 2, 1)).astype(kr_ref.dtype)


def _mla_proj(proj, cs, qn_gain, kvn_gain, wq, wk, wvt, layer):
    tm = MLA_TM
    wide = 4 * LANE
    nq = MLA_HEADS * 2 * LANE
    nk = MLA_HEADS * MLA_NOPE
    nvt = MLA_HEADS * VT_ROWS
    return pl.pallas_call(
        _mla_proj_kernel,
        grid=(TOKENS // tm,),
        in_specs=[
            pl.BlockSpec((tm, wide), lambda i: (i, CB_QC // 4)),
            pl.BlockSpec((tm, wide), lambda i: (i, CB_KVC // 4)),
            pl.BlockSpec((tm, LANE), lambda i: (i, CB_KPE)),
            pl.BlockSpec((tm, LANE), lambda i: (i, 0)),
            pl.BlockSpec((1, MLA_Q_RANK), lambda i: (0, 0)),
            pl.BlockSpec((1, MLA_KV_RANK), lambda i: (0, 0)),
            _layer_spec(wq.shape[1:], lambda i: (0, 0), layer),
            _layer_spec(wk.shape[1:], lambda i: (0, 0), layer),
            _layer_spec(wvt.shape[1:], lambda i: (0, 0), layer),
        ],
        out_specs=[
            pl.BlockSpec((tm, nq), lambda i: (i, 0)),
            pl.BlockSpec((tm, nk), lambda i: (i, 0)),
            pl.BlockSpec((nvt, tm), lambda i: (0, i)),
            pl.BlockSpec((tm, LANE), lambda i: (i, 0)),
        ],
        out_shape=[
            jax.ShapeDtypeStruct((TOKENS, nq), BF16),
            jax.ShapeDtypeStruct((TOKENS, nk), BF16),
            jax.ShapeDtypeStruct((nvt, TOKENS), BF16),
            jax.ShapeDtypeStruct((TOKENS, LANE), BF16),
        ],
        compiler_params=_cparams("parallel"),
        name="mla_proj",
    )(proj, proj, proj, cs, qn_gain, kvn_gain, wq, wk, wvt)


def _attn_kernel(q_ref, kn_ref, kr_ref, vt_ref, o_ref, s0_ref, s1_ref, m_ref, acc_ref):
    tq = ATT_TQ
    qi = pl.program_id(2)
    qs = [q_ref[:, h * 2 * LANE:(h + 1) * 2 * LANE] for h in range(ATT_HEADS)]

    def scores(j, s_ref):
        start = pl.multiple_of(j * tq, tq)
        kr = kr_ref[pl.ds(start, tq), :]
        for h in range(ATT_HEADS):
            k = jnp.concatenate([kn_ref[pl.ds(start, tq), h * LANE:(h + 1) * LANE], kr], axis=1)
            s_ref[h] = _dot_nt(k, qs[h])

    def absorb(j, s_ref, masked):
        start = pl.multiple_of(j * tq, tq)
        for h in range(ATT_HEADS):
            s = s_ref[h]
            if masked:
                ki = lax.broadcasted_iota(jnp.int32, s.shape, 0)
                qj = lax.broadcasted_iota(jnp.int32, s.shape, 1)
                s = jnp.where(ki <= qj, s, -jnp.inf)
            m = m_ref[h]
            m_new = jnp.maximum(m, jnp.max(s, axis=0, keepdims=True))
            alpha = jnp.exp(m - m_new)
            p = jnp.exp((s - m_new).astype(BF16))
            m_ref[h] = m_new
            acc_ref[h] = alpha * acc_ref[h] + _dot(vt_ref[h * VT_ROWS:(h + 1) * VT_ROWS, pl.ds(start, tq)], p)

    m_ref[...] = jnp.full(m_ref.shape, -jnp.inf, F32)
    acc_ref[...] = jnp.zeros(acc_ref.shape, F32)
    scores(0, s0_ref)

    def pair(t, carry):
        j = 2 * t
        scores(j + 1, s1_ref)
        absorb(j, s0_ref, masked=False)
        scores(j + 2, s0_ref)
        absorb(j + 1, s1_ref, masked=False)
        return carry

    lax.fori_loop(0, qi // 2, pair, 0)

    @pl.when(qi % 2 == 1)
    def _():
        scores(qi, s1_ref)
        absorb(qi - 1, s0_ref, masked=False)
        absorb(qi, s1_ref, masked=True)

    @pl.when(qi % 2 == 0)
    def _():
        absorb(qi, s0_ref, masked=True)

    for h in range(ATT_HEADS):
        acc = acc_ref[h]
        o = acc[:MLA_DV] / acc[MLA_DV:MLA_DV + 1]
        o_ref[:, h * LANE:(h + 1) * LANE] = o.T.astype(o_ref.dtype)


def _attention(q, kn, kr, vt):
    tq = ATT_TQ
    nq = SEQ // tq
    hw = ATT_HEADS * LANE
    return pl.pallas_call(
        _attn_kernel,
        grid=(BATCH, MLA_HEADS // ATT_HEADS, nq),
        in_specs=[
            pl.BlockSpec((tq, 2 * hw), lambda b, h, i: (b * nq + i, h)),
            pl.BlockSpec((SEQ, hw), lambda b, h, i: (b, h)),
            pl.BlockSpec((SEQ, LANE), lambda b, h, i: (b, 0)),
            pl.BlockSpec((ATT_HEADS * VT_ROWS, SEQ), lambda b, h, i: (h, b)),
        ],
        out_specs=pl.BlockSpec((tq, hw), lambda b, h, i: (b * nq + i, h)),
        out_shape=jax.ShapeDtypeStruct((TOKENS, MLA_HEADS * MLA_DV), F32),
        scratch_shapes=[pltpu.VMEM((ATT_HEADS, tq, tq), F32), pltpu.VMEM((ATT_HEADS, tq, tq), F32),
                        pltpu.VMEM((ATT_HEADS, 1, tq), F32), pltpu.VMEM((ATT_HEADS, VT_ROWS, tq), F32)],
        compiler_params=_cparams("parallel", "parallel", "arbitrary"),
        name="mla_attention",
    )(q, kn, kr, vt)


def _out_proj_kernel(yg_ref, yh_ref, om_ref, x_ref, mg_ref, w_ref, pg_ref, o_ref, w16_ref):
    @pl.when(pl.program_id(0) == 0)
    def _():
        w16_ref[...] = w_ref[...].astype(BF16)

    ym = _rms(om_ref[...], mg_ref[...]).astype(BF16)
    mix = jnp.concatenate([yg_ref[...], yh_ref[...], ym], axis=1)
    m = _dot(mix, w16_ref[...])
    o_ref[...] = x_ref[...] + _rms(m, pg_ref[...])


def _out_proj(y_gla, y_hg, o_mla, x, mla_gain, w_out, layer, post_gain):
    tm = OUT_TM
    d = D_MODEL
    return pl.pallas_call(
        _out_proj_kernel,
        grid=(TOKENS // tm,),
        in_specs=[
            pl.BlockSpec((tm, y_gla.shape[1]), lambda i: (i, 0)),
            pl.BlockSpec((tm, y_hg.shape[1]), lambda i: (i, 0)),
            pl.BlockSpec((tm, o_mla.shape[1]), lambda i: (i, 0)),
            pl.BlockSpec((tm, d), lambda i: (i, 0)),
            pl.BlockSpec((1, o_mla.shape[1]), lambda i: (0, 0)),
            _layer_spec(w_out.shape[1:], lambda i: (0, 0), layer, pipeline_mode=pl.Buffered(1)),
            pl.BlockSpec((1, d), lambda i: (0, 0)),
        ],
        out_specs=pl.BlockSpec((tm, d), lambda i: (i, 0)),
        out_shape=jax.ShapeDtypeStruct((TOKENS, d), F32),
        scratch_shapes=[pltpu.VMEM(w_out.shape[1:], BF16)],
        compiler_params=_cparams("arbitrary"),
        name="out_proj",
    )(y_gla, y_hg, o_mla, x, mla_gain, w_out, post_gain)


def _ffn_kernel(x_ref, ug_ref, wg_ref, wu_ref, wd_ref, pg_ref, o_ref, u_ref):
    j = pl.program_id(1)

    @pl.when(j == 0)
    def _():
        u_ref[...] = _rms(x_ref[...], ug_ref[...]).astype(BF16)
        o_ref[...] = jnp.zeros_like(o_ref)

    u = u_ref[...]
    gate = _dot(u, wg_ref[...].astype(BF16))
    up = _dot(u, wu_ref[...].astype(BF16))
    hid = (gate * _sigmoid(gate) * up).astype(BF16)
    o_ref[...] += _dot(hid, wd_ref[...].astype(BF16))

    @pl.when(j == pl.num_programs(1) - 1)
    def _():
        o_ref[...] = x_ref[...] + _rms(o_ref[...], pg_ref[...])


def _ffn(x, pre_gain, wg, wu, wd, layer, post_gain):
    tm, tf = FFN_TM, FFN_TF
    d = D_MODEL
    once = pl.Buffered(1)
    return pl.pallas_call(
        _ffn_kernel,
        grid=(TOKENS // tm, D_FF // tf),
        in_specs=[
            pl.BlockSpec((tm, d), lambda i, j: (i, 0), pipeline_mode=once),
            pl.BlockSpec((1, d), lambda i, j: (0, 0)),
            _layer_spec((d, tf), lambda i, j: (0, j), layer),
            _layer_spec((d, tf), lambda i, j: (0, j), layer),
            _layer_spec((tf, d), lambda i, j: (j, 0), layer),
            pl.BlockSpec((1, d), lambda i, j: (0, 0)),
        ],
        out_specs=pl.BlockSpec((tm, d), lambda i, j: (i, 0), pipeline_mode=once),
        out_shape=jax.ShapeDtypeStruct((TOKENS, d), F32),
        scratch_shapes=[pltpu.VMEM((tm, d), BF16)],
        compiler_params=_cparams("parallel", "arbitrary"),
        name="ffn",
    )(x, pre_gain, wg, wu, wd, post_gain)


def _rot_cols(w):
    half = MLA_ROPE // 2
    return jnp.concatenate([-w[..., half:], w[..., :half]], axis=-1)


W_IN_SIZES = (GLA_HEADS * GLA_DK, GLA_HEADS * GLA_DK, GLA_HEADS * GLA_DV, GLA_GATE_RANK,
              GLA_HEADS * GLA_DV, HGRN_HEADS * HGRN_DK, HGRN_HEADS * HGRN_DK, HGRN_HEADS * HGRN_DV,
              HGRN_HEADS * HGRN_DV, MLA_Q_RANK, MLA_KV_RANK, MLA_ROPE)
W_IN_COLS = 256


def _layout_w_in_kernel(wt_ref, o_ref):
    wt = wt_ref[...]
    cols = wt.shape[1]
    off = [int(c) for c in np.cumsum((0,) + W_IN_SIZES)]
    gq, g_low, g_out, kpe = off[0], off[3], off[4], off[11]
    half = MLA_ROPE // 2
    pe = wt[kpe:kpe + MLA_ROPE]
    rows = [wt[gq:g_low],
            wt[g_out:kpe],
            pe, -pe[half:], pe[:half],
            wt[g_low:g_out], jnp.zeros((LANE - GLA_GATE_RANK, cols), F32),
            jnp.zeros((PROJ_COLS - (CB_GLOW + 1) * LANE, cols), F32)]
    o_ref[...] = jnp.concatenate(rows, axis=0).T.astype(o_ref.dtype)


def _layout_w_in(w_in):
    depth, d, n = w_in.shape
    w_in_t = jnp.swapaxes(w_in, 1, 2)
    return pl.pallas_call(
        _layout_w_in_kernel,
        grid=(depth, d // W_IN_COLS),
        in_specs=[pl.BlockSpec((None, n, W_IN_COLS), lambda l, i: (l, 0, i))],
        out_specs=pl.BlockSpec((None, W_IN_COLS, PROJ_COLS), lambda l, i: (l, i, 0)),
        out_shape=jax.ShapeDtypeStruct((depth, d, PROJ_COLS), BF16),
        compiler_params=_cparams("parallel", "parallel"),
        name="layout_w_in",
    )(w_in_t)


def _layout_wq(wq_b):
    lead = wq_b.shape[:-1]
    w = wq_b.reshape(lead + (MLA_HEADS, MLA_NOPE + MLA_ROPE))
    nope, pe = w[..., :MLA_NOPE], w[..., MLA_NOPE:]
    w = jnp.concatenate([nope, pe, _rot_cols(pe)], axis=-1)
    return w.reshape(lead + (MLA_HEADS * 2 * LANE,)).astype(BF16)


def _layout_wkv(wkv_b):
    lead = wkv_b.shape[:-1]
    w = wkv_b.reshape(lead + (MLA_HEADS, MLA_NOPE + MLA_DV))
    k = w[..., :MLA_NOPE].reshape(lead + (MLA_HEADS * MLA_NOPE,))
    v = w[..., MLA_NOPE:].reshape(lead + (MLA_HEADS * MLA_DV,))
    return k.astype(BF16), jnp.swapaxes(v, -1, -2).astype(BF16)


def kernel(x, positions, attn_pre_norm, w_in, gla_gate_w2, gla_gate_b, gla_out_norm, hgrn_lb_logits,
           hgrn_out_norm, mla_q_norm, mla_wq_b, mla_kv_norm, mla_wkv_b, mla_out_norm, w_out,
           attn_post_norm, ffn_pre_norm, w_gate, w_up, w_down, ffn_post_norm):
    depth = w_in.shape[0]
    inv_freq = ROPE_THETA ** (-jnp.arange(0, MLA_ROPE, 2, dtype=F32) / MLA_ROPE)
    ang = positions.astype(F32).reshape(TOKENS, 1) * inv_freq
    cs = jnp.concatenate([jnp.cos(ang), jnp.cos(ang), jnp.sin(ang), jnp.sin(ang)], axis=-1)
    csum = jnp.cumsum(jax.nn.softmax(hgrn_lb_logits.astype(F32), axis=0), axis=0)
    lb = csum - csum[0]
    log_lb = jnp.log(lb)
    log_1m_lb = jnp.log1p(-lb)

    w_in_l = _layout_w_in(w_in)
    wq_l = _layout_wq(mla_wq_b)
    wk_l, wvt_l = _layout_wkv(mla_wkv_b)
    w2 = jnp.pad(gla_gate_w2, [(0, 0), (0, LANE - GLA_GATE_RANK), (0, 0)])
    b2 = gla_gate_b
    sums = jnp.asarray(_decay_sum_matrix(), dtype=BF16)

    def row(p, l):
        return p[l].reshape(1, -1)

    xf = x.reshape(TOKENS, D_MODEL)
    for l in range(depth):
        proj = _norm_matmul(xf, row(attn_pre_norm, l), w_in_l, l, F32)
        y_gla = _rec_call(
            "gla", proj,
            [(CB_GQ, 2), (CB_GK, 2), (CB_GV, 4), (CB_GLOW, 1), (CB_GOUT, 4)],
            [w2[l], row(b2, l), row(gla_out_norm, l), sums])
        y_hg = _rec_call(
            "hgrn", proj,
            [(CB_HQ, 4), (CB_HF, 4), (CB_HI, 4), (CB_HOUT, 4)],
            [row(log_lb, l), row(log_1m_lb, l), row(hgrn_out_norm, l), sums])
        q, kn, vt, kr = _mla_proj(proj, cs, row(mla_q_norm, l), row(mla_kv_norm, l), wq_l, wk_l, wvt_l, l)
        o_mla = _attention(q, kn, kr, vt)
        xf = _out_proj(y_gla, y_hg, o_mla, xf, row(mla_out_norm, l), w_out, l, row(attn_post_norm, l))
        xf = _ffn(xf, row(ffn_pre_norm, l), w_gate, w_up, w_down, l, row(ffn_post_norm, l))
    return xf.reshape(x.shape)
```

```python
import functools

import numpy as np
import jax
import jax.numpy as jnp
from jax import lax
from jax.experimental import pallas as pl
from jax.experimental.pallas import tpu as pltpu

F32 = jnp.float32
BF16 = jnp.bfloat16

D_MODEL = 2048
BATCH = 2
SEQ = 4096
DEPTH = 4
TOKENS = BATCH * SEQ
GLA_HEADS = 4
GLA_DK = 64
GLA_DV = 128
GLA_GATE_RANK = 16
GLA_GATE_NORMALIZER = 16.0
HGRN_HEADS = 4
HGRN_DK = 128
HGRN_DV = 128
MLA_HEADS = 8
MLA_Q_RANK = 512
MLA_KV_RANK = 512
MLA_NOPE = 128
MLA_ROPE = 64
MLA_DV = 128
ROPE_THETA = 10000.0
D_FF = -(-8 * D_MODEL // (3 * 256)) * 256
EPS = 1e-6
LOG2_E = float(np.log2(np.e))

LANE = 128
VMEM_LIMIT = 60 * 1024 * 1024

CB_GQ, CB_GK, CB_GV, CB_GOUT = 0, 2, 4, 8
CB_HQ, CB_HF, CB_HI, CB_HOUT = 12, 16, 20, 24
CB_QC, CB_KVC = 28, 32
CB_KPE, CB_GLOW = 36, 37
PROJ_COLS = 40 * LANE

PROJ_TM, PROJ_TN = 1024, 1280
REC_HEADS = 4
REC_CHUNK = 128
REC_TB = 1024
REC_GROUP = 8
REC_LEVELS = tuple(2 ** p for p in range(1, 8))
MLA_TM = 512
ATT_TQ = 512
ATT_HEADS = 4
VT_ROWS = MLA_DV + 16
OUT_TM = 512
FFN_TM, FFN_TF = 1024, 512


def _cparams(*sem):
    return pltpu.CompilerParams(dimension_semantics=sem, vmem_limit_bytes=VMEM_LIMIT)


def _dot(a, b):
    return jnp.dot(a, b, preferred_element_type=F32)


def _dot_nt(a, b):
    return lax.dot_general(a, b, (((1,), (1,)), ((), ())), preferred_element_type=F32)


def _dot_tn(a, b):
    return lax.dot_general(a, b, (((0,), (0,)), ((), ())), preferred_element_type=F32)


def _split_bf16(x):
    hi = x.astype(BF16)
    lo = (x - hi.astype(F32)).astype(BF16)
    return hi, lo


def _rms(x, gain):
    ms = jnp.mean(x * x, axis=-1, keepdims=True)
    return x * lax.rsqrt(ms + EPS) * gain


def _sigmoid(x):
    return 1.0 / (1.0 + jnp.exp(-x))


def _softplus_neg_abs(x):
    return jnp.log(1.0 + jnp.exp(-jnp.abs(x)))


def _log_sigmoid(x):
    return jnp.minimum(x, 0.0) - _softplus_neg_abs(x)


def _layer_spec(block, index_map, layer, **kwargs):
    return pl.BlockSpec((None,) + block, lambda *g: (layer,) + index_map(*g), **kwargs)


def _norm_matmul_kernel(x_ref, g_ref, w_ref, o_ref, h_ref):
    @pl.when(pl.program_id(1) == 0)
    def _():
        h_ref[...] = _rms(x_ref[...], g_ref[...]).astype(BF16)

    o_ref[...] = _dot(h_ref[...], w_ref[...]).astype(o_ref.dtype)


def _norm_matmul(x, gain, w, layer, out_dtype):
    m, d = x.shape
    n = w.shape[-1]
    return pl.pallas_call(
        _norm_matmul_kernel,
        grid=(m // PROJ_TM, n // PROJ_TN),
        in_specs=[
            pl.BlockSpec((PROJ_TM, d), lambda i, j: (i, 0)),
            pl.BlockSpec((1, d), lambda i, j: (0, 0)),
            _layer_spec((d, PROJ_TN), lambda i, j: (0, j), layer),
        ],
        out_specs=pl.BlockSpec((PROJ_TM, PROJ_TN), lambda i, j: (i, j)),
        out_shape=jax.ShapeDtypeStruct((m, n), out_dtype),
        scratch_shapes=[pltpu.VMEM((PROJ_TM, d), BF16)],
        compiler_params=_cparams("parallel", "arbitrary"),
        name="norm_proj",
    )(x, gain, w)


def _decay_sum_matrix():
    c = REC_CHUNK
    t = np.arange(c)
    blocks = [(t[None, :] <= t[:, None])]
    for m in REC_LEVELS:
        half = m // 2
        pos = t % m
        mid = t - pos + half
        upper = (pos >= half)[:, None] & (t[None, :] >= mid[:, None]) & (t[None, :] <= t[:, None])
        lower = (pos < half)[:, None] & (t[None, :] > t[:, None]) & (t[None, :] < mid[:, None])
        blocks.append(upper | lower)
    blocks.append(t[None, :] > t[:, None])
    sums = np.concatenate(blocks, axis=0).astype(np.float32)
    return np.concatenate([sums, sums], axis=1)


def _pair_level_codes():
    c = REC_CHUNK
    ri = lax.broadcasted_iota(jnp.int32, (c, c), 0)
    ci = lax.broadcasted_iota(jnp.int32, (c, c), 1)
    lvl = 32 - lax.clz(ri ^ ci)
    code = jnp.where(ri > ci, lvl, jnp.where(ri == ci, 0, -1))
    return code.astype(F32).astype(BF16)


def _rec_finish(o, og, gain):
    return _rms(o, gain) * (og * _sigmoid(og))


def _rec_kernel(kind, *refs):
    if kind == "gla":
        q_ref, k_ref, v_ref, gl_ref, og_ref, w2_ref, b2_ref, gain_ref, sums_ref, y_ref, st_ref = refs
    else:
        q_ref, f_ref, v_ref, og_ref, llb_ref, l1m_ref, gain_ref, sums_ref, y_ref, st_ref = refs

    @pl.when(pl.program_id(1) == 0)
    def _():
        st_ref[...] = jnp.zeros_like(st_ref)

    c = REC_CHUNK
    nl = len(REC_LEVELS)
    sums = sums_ref[...]
    gain = gain_ref[...]
    code = _pair_level_codes()
    chunks = [slice(ck * c, (ck + 1) * c) for ck in range(REC_TB // c)]
    lanes = [slice(h * LANE, (h + 1) * LANE) for h in range(REC_HEADS)]
    pack = LANE // (GLA_DK if kind == "gla" else HGRN_DK)
    qk_lanes = [slice((h // pack) * LANE, (h // pack + 1) * LANE) for h in range(REC_HEADS)]
    if pack > 1:
        lane = lax.broadcasted_iota(jnp.int32, (1, LANE), 1) // (LANE // pack)
        kmask = [jnp.where(lane == h % pack, 1.0, 0.0).astype(BF16) for h in range(REC_HEADS)]
    else:
        kmask = [None] * REC_HEADS

    def head_k(k_all, h):
        blk = k_all[:, qk_lanes[h]]
        return blk if kmask[h] is None else blk * kmask[h]

    if kind == "gla":
        w2_hi, w2_lo = _split_bf16(w2_ref[...])

    st = [st_ref[h] for h in range(REC_HEADS)]
    for g0 in range(0, len(chunks), REC_GROUP):
        group = chunks[g0:g0 + REC_GROUP]
        pairs = [(ck, h) for ck in range(len(group)) for h in range(REC_HEADS)]
        qs, ks, es = [], [], []
        for sl in group:
            if kind == "gla":
                gl_hi, gl_lo = _split_bf16(gl_ref[sl, :])
                z = _dot(gl_hi, w2_hi) + _dot(gl_lo, w2_hi) + _dot(gl_hi, w2_lo) + b2_ref[...]
                g = _log_sigmoid(z) * (1.0 / GLA_GATE_NORMALIZER)
                q = q_ref[sl, :] * (GLA_DK ** -0.5)
                k = k_ref[sl, :]
            else:
                hq = q_ref[sl, :]
                q = hq * _sigmoid(hq)
                lo = llb_ref[...]
                hi = l1m_ref[...] + _log_sigmoid(f_ref[sl, :])
                g = jnp.maximum(lo, hi) + _softplus_neg_abs(lo - hi)
                k = 1.0 - jnp.exp(g)
            g_hi, g_lo = _split_bf16(g * LOG2_E)
            e_log2 = _dot(sums, jnp.concatenate([g_hi, g_lo], axis=0))
            es.append((jnp.exp2(e_log2[0:c]), jnp.exp2(e_log2[c:(nl + 1) * c].astype(BF16)),
                       jnp.exp2(e_log2[(nl + 1) * c:(nl + 2) * c])))
            qs.append(q)
            ks.append(k)

        q16 = [q.astype(BF16) for q in qs]
        k16 = {(ck, h): head_k(ks[ck].astype(BF16), h) for ck, h in pairs}
        raw = {(ck, h): _dot_nt(q16[ck][:, qk_lanes[h]], k16[ck, h]) for ck, h in pairs}
        a16 = {p: jnp.where(code == 0, raw[p].astype(BF16), jnp.zeros((c, c), BF16)) for p in pairs}
        for lvl in range(1, nl + 1):
            e16 = [e[1][(lvl - 1) * c:lvl * c] for e in es]
            qm = [q16[ck] * e16[ck] for ck in range(len(group))]
            raw = {(ck, h): _dot_nt(qm[ck][:, qk_lanes[h]], k16[ck, h] * e16[ck][:, qk_lanes[h]]) for ck, h in pairs}
            a16 = {p: jnp.where(code == lvl, raw[p].astype(BF16), a16[p]) for p in pairs}

        for ck, sl in enumerate(group):
            e_cum, _, e_tail = es[ck]
            q_in = (qs[ck] * e_cum).astype(BF16)
            k_out = (ks[ck] * e_tail).astype(BF16)
            v16 = v_ref[sl, :].astype(BF16)
            for h in range(REC_HEADS):
                ls, kl = lanes[h], qk_lanes[h]
                o = _dot(a16[ck, h], v16[:, ls]) + _dot_nt(q_in[:, kl], st[h].astype(BF16))
                st[h] = st[h] * e_cum[c - 1:c, kl] + _dot_tn(v16[:, ls], head_k(k_out, h))
                y_ref[sl, ls] = _rec_finish(o, og_ref[sl, ls], gain).astype(y_ref.dtype)
    for h in range(REC_HEADS):
        st_ref[h] = st[h]


def _rec_call(kind, proj, col_blocks, params):
    nt = SEQ // REC_TB
    in_specs, args = [], []
    for cb, width in col_blocks:
        in_specs.append(pl.BlockSpec((REC_TB, width * LANE),
                                     lambda b, t, cb=cb, width=width: (b * nt + t, cb // width)))
        args.append(proj)
    for p in params:
        in_specs.append(pl.BlockSpec(p.shape, lambda b, t: (0, 0)))
        args.append(p)
    wide = REC_HEADS * LANE
    return pl.pallas_call(
        functools.partial(_rec_kernel, kind),
        grid=(BATCH, nt),
        in_specs=in_specs,
        out_specs=pl.BlockSpec((REC_TB, wide), lambda b, t: (b * nt + t, 0)),
        out_shape=jax.ShapeDtypeStruct((TOKENS, wide), BF16),
        scratch_shapes=[pltpu.VMEM((REC_HEADS, LANE, LANE), F32)],
        compiler_params=_cparams("parallel", "arbitrary"),
        name=kind + "_recurrence",
    )(*args)


def _mla_proj_kernel(qc_ref, kvc_ref, kpe_ref, cs_ref, qn_ref, kvn_ref, wq_ref, wk_ref, wvt_ref,
                     q_ref, kn_ref, vt_ref, kr_ref):
    scale = (MLA_NOPE + MLA_ROPE) ** -0.5
    cs = cs_ref[...]
    qn = _rms(qc_ref[...], qn_ref[...]).astype(BF16)
    q = _dot(qn, wq_ref[...])
    for h in range(MLA_HEADS):
        lo = h * 2 * LANE
        q_ref[:, lo:lo + LANE] = (q[:, lo:lo + LANE] * scale).astype(q_ref.dtype)
        q_ref[:, lo + LANE:lo + 2 * LANE] = (q[:, lo + LANE:lo + 2 * LANE] * (cs * scale)).astype(q_ref.dtype)
    kvn = _rms(kvc_ref[...], kvn_ref[...]).astype(BF16)
    kn_ref[...] = _dot(kvn, wk_ref[...]).astype(kn_ref.dtype)
    vt = _dot_nt(wvt_ref[...], kvn)
    ones = jnp.ones((VT_ROWS - MLA_DV, vt.shape[1]), vt_ref.dtype)
    for h in range(MLA_HEADS):
        vt_ref[h * VT_ROWS:h * VT_ROWS + MLA_DV, :] = vt[h * MLA_DV:(h + 1) * MLA_DV].astype(vt_ref.dtype)
        vt_ref[h * VT_ROWS + MLA_DV:(h + 1) * VT_ROWS, :] = ones
    t = kpe_ref[...] * cs
    kr_ref[...] = (t + pltpu.roll(t, LANE // 2, 1)).astype(kr_ref.dtype)


def _mla_proj(proj, cs, qn_gain, kvn_gain, wq, wk, wvt, layer):
    tm = MLA_TM
    wide = 4 * LANE
    nq = MLA_HEADS * 2 * LANE
    nk = MLA_HEADS * MLA_NOPE
    nvt = MLA_HEADS * VT_ROWS
    return pl.pallas_call(
        _mla_proj_kernel,
        grid=(TOKENS // tm,),
        in_specs=[
            pl.BlockSpec((tm, wide), lambda i: (i, CB_QC // 4)),
            pl.BlockSpec((tm, wide), lambda i: (i, CB_KVC // 4)),
            pl.BlockSpec((tm, LANE), lambda i: (i, CB_KPE)),
            pl.BlockSpec((tm, LANE), lambda i: (i, 0)),
            pl.BlockSpec((1, MLA_Q_RANK), lambda i: (0, 0)),
            pl.BlockSpec((1, MLA_KV_RANK), lambda i: (0, 0)),
            _layer_spec(wq.shape[1:], lambda i: (0, 0), layer),
            _layer_spec(wk.shape[1:], lambda i: (0, 0), layer),
            _layer_spec(wvt.shape[1:], lambda i: (0, 0), layer),
        ],
        out_specs=[
            pl.BlockSpec((tm, nq), lambda i: (i, 0)),
            pl.BlockSpec((tm, nk), lambda i: (i, 0)),
            pl.BlockSpec((nvt, tm), lambda i: (0, i)),
            pl.BlockSpec((tm, LANE), lambda i: (i, 0)),
        ],
        out_shape=[
            jax.ShapeDtypeStruct((TOKENS, nq), BF16),
            jax.ShapeDtypeStruct((TOKENS, nk), BF16),
            jax.ShapeDtypeStruct((nvt, TOKENS), BF16),
            jax.ShapeDtypeStruct((TOKENS, LANE), BF16),
        ],
        compiler_params=_cparams("parallel"),
        name="mla_proj",
    )(proj, proj, proj, cs, qn_gain, kvn_gain, wq, wk, wvt)


def _attn_kernel(q_ref, kn_ref, kr_ref, vt_ref, o_ref, s0_ref, s1_ref, m_ref, acc_ref):
    tq = ATT_TQ
    qi = pl.program_id(2)
    qs = [q_ref[:, h * 2 * LANE:(h + 1) * 2 * LANE] for h in range(ATT_HEADS)]

    def scores(j, s_ref):
        start = pl.multiple_of(j * tq, tq)
        kr = kr_ref[pl.ds(start, tq), :]
        for h in range(ATT_HEADS):
            k = jnp.concatenate([kn_ref[pl.ds(start, tq), h * LANE:(h + 1) * LANE], kr], axis=1)
            s_ref[h] = _dot_nt(k, qs[h])

    def absorb(j, s_ref, masked):
        start = pl.multiple_of(j * tq, tq)
        for h in range(ATT_HEADS):
            s = s_ref[h]
            if masked:
                ki = lax.broadcasted_iota(jnp.int32, s.shape, 0)
                qj = lax.broadcasted_iota(jnp.int32, s.shape, 1)
                s = jnp.where(ki <= qj, s, -jnp.inf)
            m = m_ref[h]
            m_new = jnp.maximum(m, jnp.max(s, axis=0, keepdims=True))
            alpha = jnp.exp(m - m_new)
            p = jnp.exp((s - m_new).astype(BF16))
            m_ref[h] = m_new
            acc_ref[h] = alpha * acc_ref[h] + _dot(vt_ref[h * VT_ROWS:(h + 1) * VT_ROWS, pl.ds(start, tq)], p)

    m_ref[...] = jnp.full(m_ref.shape, -jnp.inf, F32)
    acc_ref[...] = jnp.zeros(acc_ref.shape, F32)
    scores(0, s0_ref)

    def pair(t, carry):
        j = 2 * t
        scores(j + 1, s1_ref)
        absorb(j, s0_ref, masked=False)
        scores(j + 2, s0_ref)
        absorb(j + 1, s1_ref, masked=False)
        return carry

    lax.fori_loop(0, qi // 2, pair, 0)

    @pl.when(qi % 2 == 1)
    def _():
        scores(qi, s1_ref)
        absorb(qi - 1, s0_ref, masked=False)
        absorb(qi, s1_ref, masked=True)

    @pl.when(qi % 2 == 0)
    def _():
        absorb(qi, s0_ref, masked=True)

    for h in range(ATT_HEADS):
        acc = acc_ref[h]
        o = acc[:MLA_DV] / acc[MLA_DV:MLA_DV + 1]
        o_ref[:, h * LANE:(h + 1) * LANE] = o.T.astype(o_ref.dtype)


def _attention(q, kn, kr, vt):
    tq = ATT_TQ
    nq = SEQ // tq
    hw = ATT_HEADS * LANE
    return pl.pallas_call(
        _attn_kernel,
        grid=(BATCH, MLA_HEADS // ATT_HEADS, nq),
        in_specs=[
            pl.BlockSpec((tq, 2 * hw), lambda b, h, i: (b * nq + i, h)),
            pl.BlockSpec((SEQ, hw), lambda b, h, i: (b, h)),
            pl.BlockSpec((SEQ, LANE), lambda b, h, i: (b, 0)),
            pl.BlockSpec((ATT_HEADS * VT_ROWS, SEQ), lambda b, h, i: (h, b)),
        ],
        out_specs=pl.BlockSpec((tq, hw), lambda b, h, i: (b * nq + i, h)),
        out_shape=jax.ShapeDtypeStruct((TOKENS, MLA_HEADS * MLA_DV), F32),
        scratch_shapes=[pltpu.VMEM((ATT_HEADS, tq, tq), F32), pltpu.VMEM((ATT_HEADS, tq, tq), F32),
                        pltpu.VMEM((ATT_HEADS, 1, tq), F32), pltpu.VMEM((ATT_HEADS, VT_ROWS, tq), F32)],
        compiler_params=_cparams("parallel", "parallel", "arbitrary"),
        name="mla_attention",
    )(q, kn, kr, vt)


def _out_proj_kernel(yg_ref, yh_ref, om_ref, x_ref, mg_ref, w_ref, pg_ref, o_ref, w16_ref):
    @pl.when(pl.program_id(0) == 0)
    def _():
        w16_ref[...] = w_ref[...].astype(BF16)

    ym = _rms(om_ref[...], mg_ref[...]).astype(BF16)
    mix = jnp.concatenate([yg_ref[...], yh_ref[...], ym], axis=1)
    m = _dot(mix, w16_ref[...])
    o_ref[...] = x_ref[...] + _rms(m, pg_ref[...])


def _out_proj(y_gla, y_hg, o_mla, x, mla_gain, w_out, layer, post_gain):
    tm = OUT_TM
    d = D_MODEL
    return pl.pallas_call(
        _out_proj_kernel,
        grid=(TOKENS // tm,),
        in_specs=[
            pl.BlockSpec((tm, y_gla.shape[1]), lambda i: (i, 0)),
            pl.BlockSpec((tm, y_hg.shape[1]), lambda i: (i, 0)),
            pl.BlockSpec((tm, o_mla.shape[1]), lambda i: (i, 0)),
            pl.BlockSpec((tm, d), lambda i: (i, 0)),
            pl.BlockSpec((1, o_mla.shape[1]), lambda i: (0, 0)),
            _layer_spec(w_out.shape[1:], lambda i: (0, 0), layer, pipeline_mode=pl.Buffered(1)),
            pl.BlockSpec((1, d), lambda i: (0, 0)),
        ],
        out_specs=pl.BlockSpec((tm, d), lambda i: (i, 0)),
        out_shape=jax.ShapeDtypeStruct((TOKENS, d), F32),
        scratch_shapes=[pltpu.VMEM(w_out.shape[1:], BF16)],
        compiler_params=_cparams("arbitrary"),
        name="out_proj",
    )(y_gla, y_hg, o_mla, x, mla_gain, w_out, post_gain)


def _ffn_kernel(x_ref, ug_ref, wg_ref, wu_ref, wd_ref, pg_ref, o_ref, u_ref):
    j = pl.program_id(1)

    @pl.when(j == 0)
    def _():
        u_ref[...] = _rms(x_ref[...], ug_ref[...]).astype(BF16)
        o_ref[...] = jnp.zeros_like(o_ref)

    u = u_ref[...]
    gate = _dot(u, wg_ref[...].astype(BF16))
    up = _dot(u, wu_ref[...].astype(BF16))
    hid = (gate * _sigmoid(gate) * up).astype(BF16)
    o_ref[...] += _dot(hid, wd_ref[...].astype(BF16))

    @pl.when(j == pl.num_programs(1) - 1)
    def _():
        o_ref[...] = x_ref[...] + _rms(o_ref[...], pg_ref[...])


def _ffn(x, pre_gain, wg, wu, wd, layer, post_gain):
    tm, tf = FFN_TM, FFN_TF
    d = D_MODEL
    once = pl.Buffered(1)
    return pl.pallas_call(
        _ffn_kernel,
        grid=(TOKENS // tm, D_FF // tf),
        in_specs=[
            pl.BlockSpec((tm, d), lambda i, j: (i, 0), pipeline_mode=once),
            pl.BlockSpec((1, d), lambda i, j: (0, 0)),
            _layer_spec((d, tf), lambda i, j: (0, j), layer),
            _layer_spec((d, tf), lambda i, j: (0, j), layer),
            _layer_spec((tf, d), lambda i, j: (j, 0), layer),
            pl.BlockSpec((1, d), lambda i, j: (0, 0)),
        ],
        out_specs=pl.BlockSpec((tm, d), lambda i, j: (i, 0), pipeline_mode=once),
        out_shape=jax.ShapeDtypeStruct((TOKENS, d), F32),
        scratch_shapes=[pltpu.VMEM((tm, d), BF16)],
        compiler_params=_cparams("parallel", "arbitrary"),
        name="ffn",
    )(x, pre_gain, wg, wu, wd, post_gain)


def _rot_cols(w):
    half = MLA_ROPE // 2
    return jnp.concatenate([-w[..., half:], w[..., :half]], axis=-1)


W_IN_SIZES = (GLA_HEADS * GLA_DK, GLA_HEADS * GLA_DK, GLA_HEADS * GLA_DV, GLA_GATE_RANK,
              GLA_HEADS * GLA_DV, HGRN_HEADS * HGRN_DK, HGRN_HEADS * HGRN_DK, HGRN_HEADS * HGRN_DV,
              HGRN_HEADS * HGRN_DV, MLA_Q_RANK, MLA_KV_RANK, MLA_ROPE)
W_IN_COLS = 256


def _layout_w_in_kernel(wt_ref, o_ref):
    wt = wt_ref[...]
    cols = wt.shape[1]
    off = [int(c) for c in np.cumsum((0,) + W_IN_SIZES)]
    gq, g_low, g_out, kpe = off[0], off[3], off[4], off[11]
    half = MLA_ROPE // 2
    pe = wt[kpe:kpe + MLA_ROPE]
    rows = [wt[gq:g_low],
            wt[g_out:kpe],
            pe, -pe[half:], pe[:half],
            wt[g_low:g_out], jnp.zeros((LANE - GLA_GATE_RANK, cols), F32),
            jnp.zeros((PROJ_COLS - (CB_GLOW + 1) * LANE, cols), F32)]
    o_ref[...] = jnp.concatenate(rows, axis=0).T.astype(o_ref.dtype)


def _layout_w_in(w_in):
    depth, d, n = w_in.shape
    w_in_t = jnp.swapaxes(w_in, 1, 2)
    return pl.pallas_call(
        _layout_w_in_kernel,
        grid=(depth, d // W_IN_COLS),
        in_specs=[pl.BlockSpec((None, n, W_IN_COLS), lambda l, i: (l, 0, i))],
        out_specs=pl.BlockSpec((None, W_IN_COLS, PROJ_COLS), lambda l, i: (l, i, 0)),
        out_shape=jax.ShapeDtypeStruct((depth, d, PROJ_COLS), BF16),
        compiler_params=_cparams("parallel", "parallel"),
        name="layout_w_in",
    )(w_in_t)


def _layout_wq(wq_b):
    lead = wq_b.shape[:-1]
    w = wq_b.reshape(lead + (MLA_HEADS, MLA_NOPE + MLA_ROPE))
    nope, pe = w[..., :MLA_NOPE], w[..., MLA_NOPE:]
    w = jnp.concatenate([nope, pe, _rot_cols(pe)], axis=-1)
    return w.reshape(lead + (MLA_HEADS * 2 * LANE,)).astype(BF16)


def _layout_wkv(wkv_b):
    lead = wkv_b.shape[:-1]
    w = wkv_b.reshape(lead + (MLA_HEADS, MLA_NOPE + MLA_DV))
    k = w[..., :MLA_NOPE].reshape(lead + (MLA_HEADS * MLA_NOPE,))
    v = w[..., MLA_NOPE:].reshape(lead + (MLA_HEADS * MLA_DV,))
    return k.astype(BF16), jnp.swapaxes(v, -1, -2).astype(BF16)


def kernel(x, positions, attn_pre_norm, w_in, gla_gate_w2, gla_gate_b, gla_out_norm, hgrn_lb_logits,
           hgrn_out_norm, mla_q_norm, mla_wq_b, mla_kv_norm, mla_wkv_b, mla_out_norm, w_out,
           attn_post_norm, ffn_pre_norm, w_gate, w_up, w_down, ffn_post_norm):
    depth = w_in.shape[0]
    inv_freq = ROPE_THETA ** (-jnp.arange(0, MLA_ROPE, 2, dtype=F32) / MLA_ROPE)
    ang = positions.astype(F32).reshape(TOKENS, 1) * inv_freq
    cs = jnp.concatenate([jnp.cos(ang), jnp.cos(ang), jnp.sin(ang), jnp.sin(ang)], axis=-1)
    csum = jnp.cumsum(jax.nn.softmax(hgrn_lb_logits.astype(F32), axis=0), axis=0)
    lb = csum - csum[0]
    log_lb = jnp.log(lb)
    log_1m_lb = jnp.log1p(-lb)

    w_in_l = _layout_w_in(w_in)
    wq_l = _layout_wq(mla_wq_b)
    wk_l, wvt_l = _layout_wkv(mla_wkv_b)
    w2 = jnp.pad(gla_gate_w2, [(0, 0), (0, LANE - GLA_GATE_RANK), (0, 0)])
    b2 = gla_gate_b
    sums = jnp.asarray(_decay_sum_matrix(), dtype=BF16)

    def row(p, l):
        return p[l].reshape(1, -1)

    xf = x.reshape(TOKENS, D_MODEL)
    for l in range(depth):
        proj = _norm_matmul(xf, row(attn_pre_norm, l), w_in_l, l, F32)
        y_gla = _rec_call(
            "gla", proj,
            [(CB_GQ, 2), (CB_GK, 2), (CB_GV, 4), (CB_GLOW, 1), (CB_GOUT, 4)],
            [w2[l], row(b2, l), row(gla_out_norm, l), sums])
        y_hg = _rec_call(
            "hgrn", proj,
            [(CB_HQ, 4), (CB_HF, 4), (CB_HI, 4), (CB_HOUT, 4)],
            [row(log_lb, l), row(log_1m_lb, l), row(hgrn_out_norm, l), sums])
        q, kn, vt, kr = _mla_proj(proj, cs, row(mla_q_norm, l), row(mla_kv_norm, l), wq_l, wk_l, wvt_l, l)
        o_mla = _attention(q, kn, kr, vt)
        xf = _out_proj(y_gla, y_hg, o_mla, xf, row(mla_out_norm, l), w_out, l, row(attn_post_norm, l))
        xf = _ffn(xf, row(ffn_pre_norm, l), w_gate, w_up, w_down, l, row(ffn_post_norm, l))
    return xf.reshape(x.shape)
```

```python
import functools

import numpy as np
import jax
import jax.numpy as jnp
from jax import lax
from jax.experimental import pallas as pl
from jax.experimental.pallas import tpu as pltpu

F32 = jnp.float32
BF16 = jnp.bfloat16

D_MODEL = 2048
BATCH = 2
SEQ = 4096
DEPTH = 4
TOKENS = BATCH * SEQ
GLA_HEADS = 4
GLA_DK = 64
GLA_DV = 128
GLA_GATE_RANK = 16
GLA_GATE_NORMALIZER = 16.0
HGRN_HEADS = 4
HGRN_DK = 128
HGRN_DV = 128
MLA_HEADS = 8
MLA_Q_RANK = 512
MLA_KV_RANK = 512
MLA_NOPE = 128
MLA_ROPE = 64
MLA_DV = 128
ROPE_THETA = 10000.0
D_FF = -(-8 * D_MODEL // (3 * 256)) * 256
EPS = 1e-6
LOG2_E = float(np.log2(np.e))

LANE = 128
VMEM_LIMIT = 60 * 1024 * 1024

CB_GQ, CB_GK, CB_GV, CB_GOUT = 0, 2, 4, 8
CB_HQ, CB_HF, CB_HI, CB_HOUT = 12, 16, 20, 24
CB_QC, CB_KVC = 28, 32
CB_KPE, CB_GLOW = 36, 37
PROJ_COLS = 40 * LANE

PROJ_TM, PROJ_TN = 1024, 1280
REC_HEADS = 4
REC_CHUNK = 128
REC_TB = 1024
REC_GROUP = REC_TB // REC_CHUNK
REC_LEVELS = tuple(2 ** p for p in range(1, 8))
MLA_TM = 512
ATT_TQ = 512
ATT_HEADS = 4
VT_ROWS = MLA_DV + 16
OUT_TM = 512
FFN_TM, FFN_TF = 1024, 512


def _cparams(*sem):
    return pltpu.CompilerParams(dimension_semantics=sem, vmem_limit_bytes=VMEM_LIMIT)


def _dot(a, b):
    return jnp.dot(a, b, preferred_element_type=F32)


def _dot_nt(a, b):
    return lax.dot_general(a, b, (((1,), (1,)), ((), ())), preferred_element_type=F32)


def _dot_tn(a, b):
    return lax.dot_general(a, b, (((0,), (0,)), ((), ())), preferred_element_type=F32)


def _split_bf16(x):
    hi = x.astype(BF16)
    lo = (x - hi.astype(F32)).astype(BF16)
    return hi, lo


def _rms(x, gain):
    ms = jnp.mean(x * x, axis=-1, keepdims=True)
    return x * lax.rsqrt(ms + EPS) * gain


def _sigmoid(x):
    return 1.0 / (1.0 + jnp.exp(-x))


def _softplus_neg_abs(x):
    return jnp.log(1.0 + jnp.exp(-jnp.abs(x)))


def _log_sigmoid(x):
    return jnp.minimum(x, 0.0) - _softplus_neg_abs(x)


def _layer_spec(block, index_map, layer, **kwargs):
    return pl.BlockSpec((None,) + block, lambda *g: (layer,) + index_map(*g), **kwargs)


def _norm_matmul_kernel(x_ref, g_ref, w_ref, o_ref, h_ref):
    @pl.when(pl.program_id(1) == 0)
    def _():
        h_ref[...] = _rms(x_ref[...], g_ref[...]).astype(BF16)

    o_ref[...] = _dot(h_ref[...], w_ref[pl.program_id(1)]).astype(o_ref.dtype)


def _norm_matmul(x, gain, w, layer, out_dtype):
    m, d = x.shape
    tiles = w.shape[1]
    n = tiles * PROJ_TN
    return pl.pallas_call(
        _norm_matmul_kernel,
        grid=(m // PROJ_TM, tiles),
        in_specs=[
            pl.BlockSpec((PROJ_TM, d), lambda i, j: (i, 0)),
            pl.BlockSpec((1, d), lambda i, j: (0, 0)),
            _layer_spec((tiles, d, PROJ_TN), lambda i, j: (0, 0, 0), layer, pipeline_mode=pl.Buffered(1)),
        ],
        out_specs=pl.BlockSpec((PROJ_TM, PROJ_TN), lambda i, j: (i, j)),
        out_shape=jax.ShapeDtypeStruct((m, n), out_dtype),
        scratch_shapes=[pltpu.VMEM((PROJ_TM, d), BF16)],
        compiler_params=_cparams("parallel", "arbitrary"),
        name="norm_proj",
    )(x, gain, w)


def _decay_sum_matrix():
    c = REC_CHUNK
    t = np.arange(c)
    blocks = [(t[None, :] <= t[:, None])]
    for m in REC_LEVELS:
        half = m // 2
        pos = t % m
        mid = t - pos + half
        upper = (pos >= half)[:, None] & (t[None, :] >= mid[:, None]) & (t[None, :] <= t[:, None])
        lower = (pos < half)[:, None] & (t[None, :] > t[:, None]) & (t[None, :] < mid[:, None])
        blocks.append(upper | lower)
    blocks.append(t[None, :] > t[:, None])
    sums = np.concatenate(blocks, axis=0).astype(np.float32)
    return np.concatenate([sums, sums], axis=1)


def _pair_level_codes():
    c = REC_CHUNK
    ri = lax.broadcasted_iota(jnp.int32, (c, c), 0)
    ci = lax.broadcasted_iota(jnp.int32, (c, c), 1)
    lvl = 32 - lax.clz(ri ^ ci)
    code = jnp.where(ri > ci, lvl, jnp.where(ri == ci, 0, -1))
    return code.astype(F32).astype(BF16)


def _rec_finish(o, og, gain):
    return _rms(o, gain) * (og * _sigmoid(og))


def _rec_kernel(kind, *refs):
    if kind == "gla":
        q_ref, k_ref, v_ref, gl_ref, og_ref, w2_ref, b2_ref, gain_ref, sums_ref, y_ref, st_ref = refs
    else:
        q_ref, f_ref, v_ref, og_ref, llb_ref, l1m_ref, gain_ref, sums_ref, y_ref, st_ref = refs

    @pl.when(pl.program_id(1) == 0)
    def _():
        st_ref[...] = jnp.zeros_like(st_ref)

    c = REC_CHUNK
    nl = len(REC_LEVELS)
    sums = sums_ref[...]
    gain = gain_ref[...]
    code = _pair_level_codes()
    chunks = [slice(ck * c, (ck + 1) * c) for ck in range(REC_TB // c)]
    lanes = [slice(h * LANE, (h + 1) * LANE) for h in range(REC_HEADS)]
    pack = LANE // (GLA_DK if kind == "gla" else HGRN_DK)
    qk_lanes = [slice((h // pack) * LANE, (h // pack + 1) * LANE) for h in range(REC_HEADS)]
    if pack > 1:
        lane = lax.broadcasted_iota(jnp.int32, (1, LANE), 1) // (LANE // pack)
        kmask = [jnp.where(lane == h % pack, 1.0, 0.0).astype(BF16) for h in range(REC_HEADS)]
    else:
        kmask = [None] * REC_HEADS

    def head_k(k_all, h):
        blk = k_all[:, qk_lanes[h]]
        return blk if kmask[h] is None else blk * kmask[h]

    if kind == "gla":
        w2_hi, w2_lo = _split_bf16(w2_ref[...])

    st = [st_ref[h] for h in range(REC_HEADS)]
    for g0 in range(0, len(chunks), REC_GROUP):
        group = chunks[g0:g0 + REC_GROUP]
        pairs = [(ck, h) for ck in range(len(group)) for h in range(REC_HEADS)]
        qs, ks, es = [], [], []
        for sl in group:
            if kind == "gla":
                gl_hi, gl_lo = _split_bf16(gl_ref[sl, :])
                z = _dot(gl_hi, w2_hi) + _dot(gl_lo, w2_hi) + _dot(gl_hi, w2_lo) + b2_ref[...]
                g = _log_sigmoid(z) * (1.0 / GLA_GATE_NORMALIZER)
                q = q_ref[sl, :] * (GLA_DK ** -0.5)
                k = k_ref[sl, :]
            else:
                hq = q_ref[sl, :]
                q = hq * _sigmoid(hq)
                lo = llb_ref[...]
                hi = l1m_ref[...] + _log_sigmoid(f_ref[sl, :])
                g = jnp.maximum(lo, hi) + _softplus_neg_abs(lo - hi)
                k = 1.0 - jnp.exp(g)
            g_hi, g_lo = _split_bf16(g * LOG2_E)
            e_log2 = _dot(sums, jnp.concatenate([g_hi, g_lo], axis=0))
            es.append((jnp.exp2(e_log2[0:c]), jnp.exp2(e_log2[c:(nl + 1) * c].astype(BF16)),
                       jnp.exp2(e_log2[(nl + 1) * c:(nl + 2) * c])))
            qs.append(q)
            ks.append(k)

        q16 = [q.astype(BF16) for q in qs]
        k16 = {(ck, h): head_k(ks[ck].astype(BF16), h) for ck, h in pairs}
        raw = {(ck, h): _dot_nt(q16[ck][:, qk_lanes[h]], k16[ck, h]) for ck, h in pairs}
        a16 = {p: jnp.where(code == 0, raw[p].astype(BF16), jnp.zeros((c, c), BF16)) for p in pairs}
        for lvl in range(1, nl + 1):
            e16 = [e[1][(lvl - 1) * c:lvl * c] for e in es]
            qm = [q16[ck] * e16[ck] for ck in range(len(group))]
            raw = {(ck, h): _dot_nt(qm[ck][:, qk_lanes[h]], k16[ck, h] * e16[ck][:, qk_lanes[h]]) for ck, h in pairs}
            a16 = {p: jnp.where(code == lvl, raw[p].astype(BF16), a16[p]) for p in pairs}

        for ck, sl in enumerate(group):
            e_cum, _, e_tail = es[ck]
            q_in = (qs[ck] * e_cum).astype(BF16)
            k_out = (ks[ck] * e_tail).astype(BF16)
            v16 = v_ref[sl, :].astype(BF16)
            for h in range(REC_HEADS):
                ls, kl = lanes[h], qk_lanes[h]
                o = _dot(a16[ck, h], v16[:, ls]) + _dot_nt(q_in[:, kl], st[h].astype(BF16))
                st[h] = st[h] * e_cum[c - 1:c, kl] + _dot_tn(v16[:, ls], head_k(k_out, h))
                y_ref[sl, ls] = _rec_finish(o, og_ref[sl, ls], gain).astype(y_ref.dtype)
    for h in range(REC_HEADS):
        st_ref[h] = st[h]


def _rec_call(kind, proj, col_blocks, params):
    nt = SEQ // REC_TB
    in_specs, args = [], []
    for cb, width in col_blocks:
        in_specs.append(pl.BlockSpec((REC_TB, width * LANE),
                                     lambda b, t, cb=cb, width=width: (b * nt + t, cb // width)))
        args.append(proj)
    for p in params:
        in_specs.append(pl.BlockSpec(p.shape, lambda b, t: (0, 0)))
        args.append(p)
    wide = REC_HEADS * LANE
    return pl.pallas_call(
        functools.partial(_rec_kernel, kind),
        grid=(BATCH, nt),
        in_specs=in_specs,
        out_specs=pl.BlockSpec((REC_TB, wide), lambda b, t: (b * nt + t, 0)),
        out_shape=jax.ShapeDtypeStruct((TOKENS, wide), BF16),
        scratch_shapes=[pltpu.VMEM((REC_HEADS, LANE, LANE), F32)],
        compiler_params=_cparams("parallel", "arbitrary"),
        name=kind + "_recurrence",
    )(*args)


def _mla_proj_kernel(qc_ref, kvc_ref, kpe_ref, cs_ref, qn_ref, kvn_ref, wq_ref, wk_ref, wvt_ref,
                     q_ref, kn_ref, vt_ref, kr_ref):
    scale = (MLA_NOPE + MLA_ROPE) ** -0.5
    cs = cs_ref[...]
    qn = _rms(qc_ref[...], qn_ref[...]).astype(BF16)
    q = _dot(qn, wq_ref[...])
    for h in range(MLA_HEADS):
        lo = h * 2 * LANE
        q_ref[:, lo:lo + LANE] = (q[:, lo:lo + LANE] * scale).astype(q_ref.dtype)
        q_ref[:, lo + LANE:lo + 2 * LANE] = (q[:, lo + LANE:lo + 2 * LANE] * (cs * scale)).astype(q_ref.dtype)
    kvn = _rms(kvc_ref[...], kvn_ref[...]).astype(BF16)
    kn_ref[...] = _dot(kvn, wk_ref[...]).astype(kn_ref.dtype)
    vt = _dot_nt(wvt_ref[...], kvn)
    ones = jnp.ones((VT_ROWS - MLA_DV, vt.shape[1]), vt_ref.dtype)
    for h in range(MLA_HEADS):
        vt_ref[h * VT_ROWS:h * VT_ROWS + MLA_DV, :] = vt[h * MLA_DV:(h + 1) * MLA_DV].astype(vt_ref.dtype)
        vt_ref[h * VT_ROWS + MLA_DV:(h + 1) * VT_ROWS, :] = ones
    t = kpe_ref[...] * cs
    kr_ref[...] = (t + pltpu.roll(t, LANE // 2, 1)).astype(kr_ref.dtype)


def _mla_proj(proj, cs, qn_gain, kvn_gain, wq, wk, wvt, layer):
    tm = MLA_TM
    wide = 4 * LANE
    nq = MLA_HEADS * 2 * LANE
    nk = MLA_HEADS * MLA_NOPE
    nvt = MLA_HEADS * VT_ROWS
    return pl.pallas_call(
        _mla_proj_kernel,
        grid=(TOKENS // tm,),
        in_specs=[
            pl.BlockSpec((tm, wide), lambda i: (i, CB_QC // 4)),
            pl.BlockSpec((tm, wide), lambda i: (i, CB_KVC // 4)),
            pl.BlockSpec((tm, LANE), lambda i: (i, CB_KPE)),
            pl.BlockSpec((tm, LANE), lambda i: (i, 0)),
            pl.BlockSpec((1, MLA_Q_RANK), lambda i: (0, 0)),
            pl.BlockSpec((1, MLA_KV_RANK), lambda i: (0, 0)),
            _layer_spec(wq.shape[1:], lambda i: (0, 0), layer),
            _layer_spec(wk.shape[1:], lambda i: (0, 0), layer),
            _layer_spec(wvt.shape[1:], lambda i: (0, 0), layer),
        ],
        out_specs=[
            pl.BlockSpec((tm, nq), lambda i: (i, 0)),
            pl.BlockSpec((tm, nk), lambda i: (i, 0)),
            pl.BlockSpec((nvt, tm), lambda i: (0, i)),
            pl.BlockSpec((tm, LANE), lambda i: (i, 0)),
        ],
        out_shape=[
            jax.ShapeDtypeStruct((TOKENS, nq), BF16),
            jax.ShapeDtypeStruct((TOKENS, nk), BF16),
            jax.ShapeDtypeStruct((nvt, TOKENS), BF16),
            jax.ShapeDtypeStruct((TOKENS, LANE), BF16),
        ],
        compiler_params=_cparams("parallel"),
        name="mla_proj",
    )(proj, proj, proj, cs, qn_gain, kvn_gain, wq, wk, wvt)


def _attn_kernel(q_ref, kn_ref, kr_ref, vt_ref, o_ref, s0_ref, s1_ref, m_ref, acc_ref):
    tq = ATT_TQ
    qi = pl.program_id(2)
    qs = [q_ref[:, h * 2 * LANE:(h + 1) * 2 * LANE] for h in range(ATT_HEADS)]

    def scores(j, s_ref):
        start = pl.multiple_of(j * tq, tq)
        kr = kr_ref[pl.ds(start, tq), :]
        for h in range(ATT_HEADS):
            k = jnp.concatenate([kn_ref[pl.ds(start, tq), h * LANE:(h + 1) * LANE], kr], axis=1)
            s_ref[h] = _dot_nt(k, qs[h])

    def absorb(j, s_ref, masked):
        start = pl.multiple_of(j * tq, tq)
        for h in range(ATT_HEADS):
            s = s_ref[h]
            if masked:
                ki = lax.broadcasted_iota(jnp.int32, s.shape, 0)
                qj = lax.broadcasted_iota(jnp.int32, s.shape, 1)
                s = jnp.where(ki <= qj, s, -jnp.inf)
            m = m_ref[h]
            m_new = jnp.maximum(m, jnp.max(s, axis=0, keepdims=True))
            alpha = jnp.exp(m - m_new)
            p = jnp.exp((s - m_new).astype(BF16))
            m_ref[h] = m_new
            acc_ref[h] = alpha * acc_ref[h] + _dot(vt_ref[h * VT_ROWS:(h + 1) * VT_ROWS, pl.ds(start, tq)], p)

    m_ref[...] = jnp.full(m_ref.shape, -jnp.inf, F32)
    acc_ref[...] = jnp.zeros(acc_ref.shape, F32)
    scores(0, s0_ref)

    def pair(t, carry):
        j = 2 * t
        scores(j + 1, s1_ref)
        absorb(j, s0_ref, masked=False)
        scores(j + 2, s0_ref)
        absorb(j + 1, s1_ref, masked=False)
        return carry

    lax.fori_loop(0, qi // 2, pair, 0)

    @pl.when(qi % 2 == 1)
    def _():
        scores(qi, s1_ref)
        absorb(qi - 1, s0_ref, masked=False)
        absorb(qi, s1_ref, masked=True)

    @pl.when(qi % 2 == 0)
    def _():
        absorb(qi, s0_ref, masked=True)

    for h in range(ATT_HEADS):
        acc = acc_ref[h]
        o = acc[:MLA_DV] / acc[MLA_DV:MLA_DV + 1]
        o_ref[:, h * LANE:(h + 1) * LANE] = o.T.astype(o_ref.dtype)


def _attention(q, kn, kr, vt):
    tq = ATT_TQ
    nq = SEQ // tq
    hw = ATT_HEADS * LANE
    return pl.pallas_call(
        _attn_kernel,
        grid=(BATCH, MLA_HEADS // ATT_HEADS, nq),
        in_specs=[
            pl.BlockSpec((tq, 2 * hw), lambda b, h, i: (b * nq + i, h)),
            pl.BlockSpec((SEQ, hw), lambda b, h, i: (b, h)),
            pl.BlockSpec((SEQ, LANE), lambda b, h, i: (b, 0)),
            pl.BlockSpec((ATT_HEADS * VT_ROWS, SEQ), lambda b, h, i: (h, b)),
        ],
        out_specs=pl.BlockSpec((tq, hw), lambda b, h, i: (b * nq + i, h)),
        out_shape=jax.ShapeDtypeStruct((TOKENS, MLA_HEADS * MLA_DV), F32),
        scratch_shapes=[pltpu.VMEM((ATT_HEADS, tq, tq), F32), pltpu.VMEM((ATT_HEADS, tq, tq), F32),
                        pltpu.VMEM((ATT_HEADS, 1, tq), F32), pltpu.VMEM((ATT_HEADS, VT_ROWS, tq), F32)],
        compiler_params=_cparams("parallel", "parallel", "arbitrary"),
        name="mla_attention",
    )(q, kn, kr, vt)


def _out_proj_kernel(yg_ref, yh_ref, om_ref, x_ref, mg_ref, w_ref, pg_ref, o_ref, w16_ref):
    @pl.when(pl.program_id(0) == 0)
    def _():
        w16_ref[...] = w_ref[...].astype(BF16)

    ym = _rms(om_ref[...], mg_ref[...]).astype(BF16)
    mix = jnp.concatenate([yg_ref[...], yh_ref[...], ym], axis=1)
    m = _dot(mix, w16_ref[...])
    o_ref[...] = x_ref[...] + _rms(m, pg_ref[...])


def _out_proj(y_gla, y_hg, o_mla, x, mla_gain, w_out, layer, post_gain):
    tm = OUT_TM
    d = D_MODEL
    return pl.pallas_call(
        _out_proj_kernel,
        grid=(TOKENS // tm,),
        in_specs=[
            pl.BlockSpec((tm, y_gla.shape[1]), lambda i: (i, 0)),
            pl.BlockSpec((tm, y_hg.shape[1]), lambda i: (i, 0)),
            pl.BlockSpec((tm, o_mla.shape[1]), lambda i: (i, 0)),
            pl.BlockSpec((tm, d), lambda i: (i, 0)),
            pl.BlockSpec((1, o_mla.shape[1]), lambda i: (0, 0)),
            _layer_spec(w_out.shape[1:], lambda i: (0, 0), layer, pipeline_mode=pl.Buffered(1)),
            pl.BlockSpec((1, d), lambda i: (0, 0)),
        ],
        out_specs=pl.BlockSpec((tm, d), lambda i: (i, 0)),
        out_shape=jax.ShapeDtypeStruct((TOKENS, d), F32),
        scratch_shapes=[pltpu.VMEM(w_out.shape[1:], BF16)],
        compiler_params=_cparams("arbitrary"),
        name="out_proj",
    )(y_gla, y_hg, o_mla, x, mla_gain, w_out, post_gain)


def _ffn_kernel(x_ref, ug_ref, wg_ref, wu_ref, wd_ref, pg_ref, o_ref, u_ref):
    j = pl.program_id(1)

    @pl.when(j == 0)
    def _():
        u_ref[...] = _rms(x_ref[...], ug_ref[...]).astype(BF16)
        o_ref[...] = jnp.zeros_like(o_ref)

    u = u_ref[...]
    gate = _dot(u, wg_ref[...].astype(BF16))
    up = _dot(u, wu_ref[...].astype(BF16))
    hid = (gate * _sigmoid(gate) * up).astype(BF16)
    o_ref[...] += _dot(hid, wd_ref[...].astype(BF16))

    @pl.when(j == pl.num_programs(1) - 1)
    def _():
        o_ref[...] = x_ref[...] + _rms(o_ref[...], pg_ref[...])


def _ffn(x, pre_gain, wg, wu, wd, layer, post_gain):
    tm, tf = FFN_TM, FFN_TF
    d = D_MODEL
    once = pl.Buffered(1)
    return pl.pallas_call(
        _ffn_kernel,
        grid=(TOKENS // tm, D_FF // tf),
        in_specs=[
            pl.BlockSpec((tm, d), lambda i, j: (i, 0), pipeline_mode=once),
            pl.BlockSpec((1, d), lambda i, j: (0, 0)),
            _layer_spec((d, tf), lambda i, j: (0, j), layer),
            _layer_spec((d, tf), lambda i, j: (0, j), layer),
            _layer_spec((tf, d), lambda i, j: (j, 0), layer),
            pl.BlockSpec((1, d), lambda i, j: (0, 0)),
        ],
        out_specs=pl.BlockSpec((tm, d), lambda i, j: (i, 0), pipeline_mode=once),
        out_shape=jax.ShapeDtypeStruct((TOKENS, d), F32),
        scratch_shapes=[pltpu.VMEM((tm, d), BF16)],
        compiler_params=_cparams("parallel", "arbitrary"),
        name="ffn",
    )(x, pre_gain, wg, wu, wd, post_gain)


def _rot_cols(w):
    half = MLA_ROPE // 2
    return jnp.concatenate([-w[..., half:], w[..., :half]], axis=-1)


W_IN_SIZES = (GLA_HEADS * GLA_DK, GLA_HEADS * GLA_DK, GLA_HEADS * GLA_DV, GLA_GATE_RANK,
              GLA_HEADS * GLA_DV, HGRN_HEADS * HGRN_DK, HGRN_HEADS * HGRN_DK, HGRN_HEADS * HGRN_DV,
              HGRN_HEADS * HGRN_DV, MLA_Q_RANK, MLA_KV_RANK, MLA_ROPE)
W_IN_COLS = 256


def _layout_w_in_kernel(wt_ref, o_ref):
    wt = wt_ref[...]
    cols = wt.shape[1]
    off = [int(c) for c in np.cumsum((0,) + W_IN_SIZES)]
    gq, g_low, g_out, kpe = off[0], off[3], off[4], off[11]
    half = MLA_ROPE // 2
    pe = wt[kpe:kpe + MLA_ROPE]
    rows = [wt[gq:g_low],
            wt[g_out:kpe],
            pe, -pe[half:], pe[:half],
            wt[g_low:g_out], jnp.zeros((LANE - GLA_GATE_RANK, cols), F32),
            jnp.zeros((PROJ_COLS - (CB_GLOW + 1) * LANE, cols), F32)]
    w = jnp.concatenate(rows, axis=0).T.astype(o_ref.dtype)
    for t in range(PROJ_COLS // PROJ_TN):
        o_ref[t] = w[:, t * PROJ_TN:(t + 1) * PROJ_TN]


def _layout_w_in(w_in):
    depth, d, n = w_in.shape
    tiles = PROJ_COLS // PROJ_TN
    w_in_t = jnp.swapaxes(w_in, 1, 2)
    return pl.pallas_call(
        _layout_w_in_kernel,
        grid=(depth, d // W_IN_COLS),
        in_specs=[pl.BlockSpec((None, n, W_IN_COLS), lambda l, i: (l, 0, i))],
        out_specs=pl.BlockSpec((None, tiles, W_IN_COLS, PROJ_TN), lambda l, i: (l, 0, i, 0)),
        out_shape=jax.ShapeDtypeStruct((depth, tiles, d, PROJ_TN), BF16),
        compiler_params=_cparams("parallel", "parallel"),
        name="layout_w_in",
    )(w_in_t)


def _layout_wq(wq_b):
    lead = wq_b.shape[:-1]
    w = wq_b.reshape(lead + (MLA_HEADS, MLA_NOPE + MLA_ROPE))
    nope, pe = w[..., :MLA_NOPE], w[..., MLA_NOPE:]
    w = jnp.concatenate([nope, pe, _rot_cols(pe)], axis=-1)
    return w.reshape(lead + (MLA_HEADS * 2 * LANE,)).astype(BF16)


def _layout_wkv(wkv_b):
    lead = wkv_b.shape[:-1]
    w = wkv_b.reshape(lead + (MLA_HEADS, MLA_NOPE + MLA_DV))
    k = w[..., :MLA_NOPE].reshape(lead + (MLA_HEADS * MLA_NOPE,))
    v = w[..., MLA_NOPE:].reshape(lead + (MLA_HEADS * MLA_DV,))
    return k.astype(BF16), jnp.swapaxes(v, -1, -2).astype(BF16)


def kernel(x, positions, attn_pre_norm, w_in, gla_gate_w2, gla_gate_b, gla_out_norm, hgrn_lb_logits,
           hgrn_out_norm, mla_q_norm, mla_wq_b, mla_kv_norm, mla_wkv_b, mla_out_norm, w_out,
           attn_post_norm, ffn_pre_norm, w_gate, w_up, w_down, ffn_post_norm):
    depth = w_in.shape[0]
    inv_freq = ROPE_THETA ** (-jnp.arange(0, MLA_ROPE, 2, dtype=F32) / MLA_ROPE)
    ang = positions.astype(F32).reshape(TOKENS, 1) * inv_freq
    cs = jnp.concatenate([jnp.cos(ang), jnp.cos(ang), jnp.sin(ang), jnp.sin(ang)], axis=-1)
    csum = jnp.cumsum(jax.nn.softmax(hgrn_lb_logits.astype(F32), axis=0), axis=0)
    lb = csum - csum[0]
    log_lb = jnp.log(lb)
    log_1m_lb = jnp.log1p(-lb)

    w_in_l = _layout_w_in(w_in)
    wq_l = _layout_wq(mla_wq_b)
    wk_l, wvt_l = _layout_wkv(mla_wkv_b)
    w2 = jnp.pad(gla_gate_w2, [(0, 0), (0, LANE - GLA_GATE_RANK), (0, 0)])
    b2 = gla_gate_b
    sums = jnp.asarray(_decay_sum_matrix(), dtype=BF16)

    def row(p, l):
        return p[l].reshape(1, -1)

    xf = x.reshape(TOKENS, D_MODEL)
    for l in range(depth):
        proj = _norm_matmul(xf, row(attn_pre_norm, l), w_in_l, l, F32)
        y_gla = _rec_call(
            "gla", proj,
            [(CB_GQ, 2), (CB_GK, 2), (CB_GV, 4), (CB_GLOW, 1), (CB_GOUT, 4)],
            [w2[l], row(b2, l), row(gla_out_norm, l), sums])
        y_hg = _rec_call(
            "hgrn", proj,
            [(CB_HQ, 4), (CB_HF, 4), (CB_HI, 4), (CB_HOUT, 4)],
            [row(log_lb, l), row(log_1m_lb, l), row(hgrn_out_norm, l), sums])
        q, kn, vt, kr = _mla_proj(proj, cs, row(mla_q_norm, l), row(mla_kv_norm, l), wq_l, wk_l, wvt_l, l)
        o_mla = _attention(q, kn, kr, vt)
        xf = _out_proj(y_gla, y_hg, o_mla, xf, row(mla_out_norm, l), w_out, l, row(attn_post_norm, l))
        xf = _ffn(xf, row(ffn_pre_norm, l), w_gate, w_up, w_down, l, row(ffn_post_norm, l))
    return xf.reshape(x.shape)
```

```python
import functools

import numpy as np
import jax
import jax.numpy as jnp
from jax import lax
from jax.experimental import pallas as pl
from jax.experimental.pallas import tpu as pltpu

F32 = jnp.float32
BF16 = jnp.bfloat16

D_MODEL = 2048
BATCH = 2
SEQ = 4096
DEPTH = 4
TOKENS = BATCH * SEQ
GLA_HEADS = 4
GLA_DK = 64
GLA_DV = 128
GLA_GATE_RANK = 16
GLA_GATE_NORMALIZER = 16.0
HGRN_HEADS = 4
HGRN_DK = 128
HGRN_DV = 128
MLA_HEADS = 8
MLA_Q_RANK = 512
MLA_KV_RANK = 512
MLA_NOPE = 128
MLA_ROPE = 64
MLA_DV = 128
ROPE_THETA = 10000.0
D_FF = -(-8 * D_MODEL // (3 * 256)) * 256
EPS = 1e-6
LOG2_E = float(np.log2(np.e))

LANE = 128
VMEM_LIMIT = 60 * 1024 * 1024

CB_GQ, CB_GK, CB_GV, CB_GOUT = 0, 2, 4, 8
CB_HQ, CB_HF, CB_HI, CB_HOUT = 12, 16, 20, 24
CB_QC, CB_KVC = 28, 32
CB_KPE, CB_GLOW = 36, 37
PROJ_COLS = 40 * LANE

PROJ_TM, PROJ_TN = 1024, 1280
REC_HEADS = 4
REC_CHUNK = 128
REC_TB = 1024
REC_GROUP = REC_TB // REC_CHUNK
REC_LEVELS = tuple(2 ** p for p in range(1, 8))
MLA_TM = 512
ATT_TQ = 512
ATT_HEADS = 4
VT_ROWS = MLA_DV + 16
OUT_TM = 512
FFN_TM, FFN_TF = 1024, 512


def _cparams(*sem):
    return pltpu.CompilerParams(dimension_semantics=sem, vmem_limit_bytes=VMEM_LIMIT)


def _dot(a, b):
    return jnp.dot(a, b, preferred_element_type=F32)


def _dot_nt(a, b):
    return lax.dot_general(a, b, (((1,), (1,)), ((), ())), preferred_element_type=F32)


def _dot_tn(a, b):
    return lax.dot_general(a, b, (((0,), (0,)), ((), ())), preferred_element_type=F32)


def _split_bf16(x):
    hi = x.astype(BF16)
    lo = (x - hi.astype(F32)).astype(BF16)
    return hi, lo


def _rms(x, gain):
    ms = jnp.mean(x * x, axis=-1, keepdims=True)
    return x * lax.rsqrt(ms + EPS) * gain


def _sigmoid(x):
    return 1.0 / (1.0 + jnp.exp(-x))


def _softplus_neg_abs(x):
    return jnp.log(1.0 + jnp.exp(-jnp.abs(x)))


def _log_sigmoid(x):
    return jnp.minimum(x, 0.0) - _softplus_neg_abs(x)


def _layer_spec(block, index_map, layer, **kwargs):
    return pl.BlockSpec((None,) + block, lambda *g: (layer,) + index_map(*g), **kwargs)


def _norm_matmul_kernel(x_ref, g_ref, w_ref, o_ref, h_ref):
    @pl.when(pl.program_id(1) == 0)
    def _():
        h_ref[...] = _rms(x_ref[...], g_ref[...]).astype(BF16)

    o_ref[...] = _dot(h_ref[...], w_ref[pl.program_id(1)]).astype(o_ref.dtype)


def _norm_matmul(x, gain, w, layer, out_dtype):
    m, d = x.shape
    tiles = w.shape[1]
    n = tiles * PROJ_TN
    return pl.pallas_call(
        _norm_matmul_kernel,
        grid=(m // PROJ_TM, tiles),
        in_specs=[
            pl.BlockSpec((PROJ_TM, d), lambda i, j: (i, 0)),
            pl.BlockSpec((1, d), lambda i, j: (0, 0)),
            _layer_spec((tiles, d, PROJ_TN), lambda i, j: (0, 0, 0), layer, pipeline_mode=pl.Buffered(1)),
        ],
        out_specs=pl.BlockSpec((PROJ_TM, PROJ_TN), lambda i, j: (i, j)),
        out_shape=jax.ShapeDtypeStruct((m, n), out_dtype),
        scratch_shapes=[pltpu.VMEM((PROJ_TM, d), BF16)],
        compiler_params=_cparams("parallel", "arbitrary"),
        name="norm_proj",
    )(x, gain, w)


def _decay_sum_matrix():
    c = REC_CHUNK
    t = np.arange(c)
    blocks = [(t[None, :] <= t[:, None])]
    for m in REC_LEVELS:
        half = m // 2
        pos = t % m
        mid = t - pos + half
        upper = (pos >= half)[:, None] & (t[None, :] >= mid[:, None]) & (t[None, :] <= t[:, None])
        lower = (pos < half)[:, None] & (t[None, :] > t[:, None]) & (t[None, :] < mid[:, None])
        blocks.append(upper | lower)
    blocks.append(t[None, :] > t[:, None])
    sums = np.concatenate(blocks, axis=0).astype(np.float32)
    return np.concatenate([sums, sums], axis=1)


def _pair_level_codes():
    c = REC_CHUNK
    ri = lax.broadcasted_iota(jnp.int32, (c, c), 0)
    ci = lax.broadcasted_iota(jnp.int32, (c, c), 1)
    lvl = 32 - lax.clz(ri ^ ci)
    code = jnp.where(ri > ci, lvl, jnp.where(ri == ci, 0, -1))
    return code.astype(F32).astype(BF16)


def _rec_finish(o, og, gain):
    return _rms(o, gain) * (og * _sigmoid(og))


def _rec_kernel(kind, *refs):
    if kind == "gla":
        q_ref, k_ref, v_ref, gl_ref, og_ref, w2_ref, b2_ref, gain_ref, sums_ref, y_ref, st_ref = refs
    else:
        q_ref, f_ref, v_ref, og_ref, llb_ref, l1m_ref, gain_ref, sums_ref, y_ref, st_ref = refs

    @pl.when(pl.program_id(1) == 0)
    def _():
        st_ref[...] = jnp.zeros_like(st_ref)

    c = REC_CHUNK
    nl = len(REC_LEVELS)
    sums = sums_ref[...]
    gain = gain_ref[...]
    code = _pair_level_codes()
    chunks = [slice(ck * c, (ck + 1) * c) for ck in range(REC_TB // c)]
    lanes = [slice(h * LANE, (h + 1) * LANE) for h in range(REC_HEADS)]
    pack = LANE // (GLA_DK if kind == "gla" else HGRN_DK)
    qk_lanes = [slice((h // pack) * LANE, (h // pack + 1) * LANE) for h in range(REC_HEADS)]
    if pack > 1:
        lane = lax.broadcasted_iota(jnp.int32, (1, LANE), 1) // (LANE // pack)
        kmask = [jnp.where(lane == h % pack, 1.0, 0.0).astype(BF16) for h in range(REC_HEADS)]
    else:
        kmask = [None] * REC_HEADS

    def head_k(k_all, h):
        blk = k_all[:, qk_lanes[h]]
        return blk if kmask[h] is None else blk * kmask[h]

    if kind == "gla":
        w2_hi, w2_lo = _split_bf16(w2_ref[...])

    st = [st_ref[h] for h in range(REC_HEADS)]
    for g0 in range(0, len(chunks), REC_GROUP):
        group = chunks[g0:g0 + REC_GROUP]
        pairs = [(ck, h) for ck in range(len(group)) for h in range(REC_HEADS)]
        qs, ks, es = [], [], []
        for sl in group:
            if kind == "gla":
                gl_hi, gl_lo = _split_bf16(gl_ref[sl, :])
                z = _dot(gl_hi, w2_hi) + _dot(gl_lo, w2_hi) + _dot(gl_hi, w2_lo) + b2_ref[...]
                g = _log_sigmoid(z) * (1.0 / GLA_GATE_NORMALIZER)
                q = q_ref[sl, :] * (GLA_DK ** -0.5)
                k = k_ref[sl, :]
            else:
                hq = q_ref[sl, :]
                q = hq * _sigmoid(hq)
                lo = llb_ref[...]
                hi = l1m_ref[...] + _log_sigmoid(f_ref[sl, :])
                g = jnp.maximum(lo, hi) + _softplus_neg_abs(lo - hi)
                k = 1.0 - jnp.exp(g)
            g_hi, g_lo = _split_bf16(g * LOG2_E)
            e_log2 = _dot(sums, jnp.concatenate([g_hi, g_lo], axis=0))
            es.append((jnp.exp2(e_log2[0:c]), jnp.exp2(e_log2[c:(nl + 1) * c].astype(BF16)),
                       jnp.exp2(e_log2[(nl + 1) * c:(nl + 2) * c])))
            qs.append(q)
            ks.append(k)

        q16 = [q.astype(BF16) for q in qs]
        k16 = {(ck, h): head_k(ks[ck].astype(BF16), h) for ck, h in pairs}
        raw = {(ck, h): _dot_nt(q16[ck][:, qk_lanes[h]], k16[ck, h]) for ck, h in pairs}
        a16 = {p: jnp.where(code == 0, raw[p].astype(BF16), jnp.zeros((c, c), BF16)) for p in pairs}
        for lvl in range(1, nl + 1):
            e16 = [e[1][(lvl - 1) * c:lvl * c] for e in es]
            qm = [q16[ck] * e16[ck] for ck in range(len(group))]
            raw = {(ck, h): _dot_nt(qm[ck][:, qk_lanes[h]], k16[ck, h] * e16[ck][:, qk_lanes[h]]) for ck, h in pairs}
            a16 = {p: jnp.where(code == lvl, raw[p].astype(BF16), a16[p]) for p in pairs}

        for ck, sl in enumerate(group):
            e_cum, _, e_tail = es[ck]
            q_in = (qs[ck] * e_cum).astype(BF16)
            k_out = (ks[ck] * e_tail).astype(BF16)
            v16 = v_ref[sl, :].astype(BF16)
            for h in range(REC_HEADS):
                ls, kl = lanes[h], qk_lanes[h]
                o = _dot(a16[ck, h], v16[:, ls]) + _dot_nt(q_in[:, kl], st[h].astype(BF16))
                st[h] = st[h] * e_cum[c - 1:c, kl] + _dot_tn(v16[:, ls], head_k(k_out, h))
                y_ref[sl, ls] = _rec_finish(o, og_ref[sl, ls], gain).astype(y_ref.dtype)
    for h in range(REC_HEADS):
        st_ref[h] = st[h]


def _rec_call(kind, proj, col_blocks, params):
    nt = SEQ // REC_TB
    in_specs, args = [], []
    for cb, width in col_blocks:
        in_specs.append(pl.BlockSpec((REC_TB, width * LANE),
                                     lambda b, t, cb=cb, width=width: (b * nt + t, cb // width)))
        args.append(proj)
    for p in params:
        in_specs.append(pl.BlockSpec(p.shape, lambda b, t: (0, 0)))
        args.append(p)
    wide = REC_HEADS * LANE
    return pl.pallas_call(
        functools.partial(_rec_kernel, kind),
        grid=(BATCH, nt),
        in_specs=in_specs,
        out_specs=pl.BlockSpec((REC_TB, wide), lambda b, t: (b * nt + t, 0)),
        out_shape=jax.ShapeDtypeStruct((TOKENS, wide), BF16),
        scratch_shapes=[pltpu.VMEM((REC_HEADS, LANE, LANE), F32)],
        compiler_params=_cparams("parallel", "arbitrary"),
        name=kind + "_recurrence",
    )(*args)


def _mla_proj_kernel(qc_ref, kvc_ref, kpe_ref, cs_ref, qn_ref, kvn_ref, wq_ref, wk_ref, wvt_ref,
                     q_ref, kn_ref, vt_ref, kr_ref):
    scale = (MLA_NOPE + MLA_ROPE) ** -0.5
    cs = cs_ref[...]
    qn = _rms(qc_ref[...], qn_ref[...]).astype(BF16)
    q = _dot(qn, wq_ref[...])
    for h in range(MLA_HEADS):
        lo = h * 2 * LANE
        q_ref[:, lo:lo + LANE] = (q[:, lo:lo + LANE] * scale).astype(q_ref.dtype)
        q_ref[:, lo + LANE:lo + 2 * LANE] = (q[:, lo + LANE:lo + 2 * LANE] * (cs * scale)).astype(q_ref.dtype)
    kvn = _rms(kvc_ref[...], kvn_ref[...]).astype(BF16)
    kn_ref[...] = _dot(kvn, wk_ref[...]).astype(kn_ref.dtype)
    vt = _dot_nt(wvt_ref[...], kvn)
    ones = jnp.ones((VT_ROWS - MLA_DV, vt.shape[1]), vt_ref.dtype)
    for h in range(MLA_HEADS):
        vt_ref[h * VT_ROWS:h * VT_ROWS + MLA_DV, :] = vt[h * MLA_DV:(h + 1) * MLA_DV].astype(vt_ref.dtype)
        vt_ref[h * VT_ROWS + MLA_DV:(h + 1) * VT_ROWS, :] = ones
    t = kpe_ref[...] * cs
    kr_ref[...] = (t + pltpu.roll(t, LANE // 2, 1)).astype(kr_ref.dtype)


def _mla_proj(proj, cs, qn_gain, kvn_gain, wq, wk, wvt, layer):
    tm = MLA_TM
    wide = 4 * LANE
    nq = MLA_HEADS * 2 * LANE
    nk = MLA_HEADS * MLA_NOPE
    nvt = MLA_HEADS * VT_ROWS
    return pl.pallas_call(
        _mla_proj_kernel,
        grid=(TOKENS // tm,),
        in_specs=[
            pl.BlockSpec((tm, wide), lambda i: (i, CB_QC // 4)),
            pl.BlockSpec((tm, wide), lambda i: (i, CB_KVC // 4)),
            pl.BlockSpec((tm, LANE), lambda i: (i, CB_KPE)),
            pl.BlockSpec((tm, LANE), lambda i: (i, 0)),
            pl.BlockSpec((1, MLA_Q_RANK), lambda i: (0, 0)),
            pl.BlockSpec((1, MLA_KV_RANK), lambda i: (0, 0)),
            _layer_spec(wq.shape[1:], lambda i: (0, 0), layer),
            _layer_spec(wk.shape[1:], lambda i: (0, 0), layer),
            _layer_spec(wvt.shape[1:], lambda i: (0, 0), layer),
        ],
        out_specs=[
            pl.BlockSpec((tm, nq), lambda i: (i, 0)),
            pl.BlockSpec((tm, nk), lambda i: (i, 0)),
            pl.BlockSpec((nvt, tm), lambda i: (0, i)),
            pl.BlockSpec((tm, LANE), lambda i: (i, 0)),
        ],
        out_shape=[
            jax.ShapeDtypeStruct((TOKENS, nq), BF16),
            jax.ShapeDtypeStruct((TOKENS, nk), BF16),
            jax.ShapeDtypeStruct((nvt, TOKENS), BF16),
            jax.ShapeDtypeStruct((TOKENS, LANE), BF16),
        ],
        compiler_params=_cparams("parallel"),
        name="mla_proj",
    )(proj, proj, proj, cs, qn_gain, kvn_gain, wq, wk, wvt)


def _attn_kernel(q_ref, kn_ref, kr_ref, vt_ref, o_ref, s0_ref, s1_ref, m_ref, acc_ref):
    tq = ATT_TQ
    qi = pl.program_id(2)
    qs = [q_ref[:, h * 2 * LANE:(h + 1) * 2 * LANE] for h in range(ATT_HEADS)]

    def scores(j, s_ref):
        start = pl.multiple_of(j * tq, tq)
        kr = kr_ref[pl.ds(start, tq), :]
        for h in range(ATT_HEADS):
            k = jnp.concatenate([kn_ref[pl.ds(start, tq), h * LANE:(h + 1) * LANE], kr], axis=1)
            s_ref[h] = _dot_nt(k, qs[h])

    def absorb(j, s_ref, masked):
        start = pl.multiple_of(j * tq, tq)
        for h in range(ATT_HEADS):
            s = s_ref[h]
            if masked:
                ki = lax.broadcasted_iota(jnp.int32, s.shape, 0)
                qj = lax.broadcasted_iota(jnp.int32, s.shape, 1)
                s = jnp.where(ki <= qj, s, -jnp.inf)
            m = m_ref[h]
            m_new = jnp.maximum(m, jnp.max(s, axis=0, keepdims=True))
            alpha = jnp.exp(m - m_new)
            p = jnp.exp((s - m_new).astype(BF16))
            m_ref[h] = m_new
            acc_ref[h] = alpha * acc_ref[h] + _dot(vt_ref[h * VT_ROWS:(h + 1) * VT_ROWS, pl.ds(start, tq)], p)

    m_ref[...] = jnp.full(m_ref.shape, -jnp.inf, F32)
    acc_ref[...] = jnp.zeros(acc_ref.shape, F32)
    scores(0, s0_ref)

    def pair(t, carry):
        j = 2 * t
        scores(j + 1, s1_ref)
        absorb(j, s0_ref, masked=False)
        scores(j + 2, s0_ref)
        absorb(j + 1, s1_ref, masked=False)
        return carry

    lax.fori_loop(0, qi // 2, pair, 0)

    @pl.when(qi % 2 == 1)
    def _():
        scores(qi, s1_ref)
        absorb(qi - 1, s0_ref, masked=False)
        absorb(qi, s1_ref, masked=True)

    @pl.when(qi % 2 == 0)
    def _():
        absorb(qi, s0_ref, masked=True)

    for h in range(ATT_HEADS):
        acc = acc_ref[h]
        o = acc[:MLA_DV] / acc[MLA_DV:MLA_DV + 1]
        o_ref[:, h * LANE:(h + 1) * LANE] = o.T.astype(o_ref.dtype)


def _attention(q, kn, kr, vt):
    tq = ATT_TQ
    nq = SEQ // tq
    hw = ATT_HEADS * LANE
    return pl.pallas_call(
        _attn_kernel,
        grid=(BATCH, MLA_HEADS // ATT_HEADS, nq),
        in_specs=[
            pl.BlockSpec((tq, 2 * hw), lambda b, h, i: (b * nq + i, h)),
            pl.BlockSpec((SEQ, hw), lambda b, h, i: (b, h)),
            pl.BlockSpec((SEQ, LANE), lambda b, h, i: (b, 0)),
            pl.BlockSpec((ATT_HEADS * VT_ROWS, SEQ), lambda b, h, i: (h, b)),
        ],
        out_specs=pl.BlockSpec((tq, hw), lambda b, h, i: (b * nq + i, h)),
        out_shape=jax.ShapeDtypeStruct((TOKENS, MLA_HEADS * MLA_DV), F32),
        scratch_shapes=[pltpu.VMEM((ATT_HEADS, tq, tq), F32), pltpu.VMEM((ATT_HEADS, tq, tq), F32),
                        pltpu.VMEM((ATT_HEADS, 1, tq), F32), pltpu.VMEM((ATT_HEADS, VT_ROWS, tq), F32)],
        compiler_params=_cparams("parallel", "parallel", "arbitrary"),
        name="mla_attention",
    )(q, kn, kr, vt)


def _out_proj_kernel(yg_ref, yh_ref, om_ref, x_ref, mg_ref, w_ref, pg_ref, o_ref, w16_ref):
    @pl.when(pl.program_id(0) == 0)
    def _():
        w16_ref[...] = w_ref[...].astype(BF16)

    ym = _rms(om_ref[...], mg_ref[...]).astype(BF16)
    mix = jnp.concatenate([yg_ref[...], yh_ref[...], ym], axis=1)
    m = _dot(mix, w16_ref[...])
    o_ref[...] = x_ref[...] + _rms(m, pg_ref[...])


def _out_proj(y_gla, y_hg, o_mla, x, mla_gain, w_out, layer, post_gain):
    tm = OUT_TM
    d = D_MODEL
    return pl.pallas_call(
        _out_proj_kernel,
        grid=(TOKENS // tm,),
        in_specs=[
            pl.BlockSpec((tm, y_gla.shape[1]), lambda i: (i, 0)),
            pl.BlockSpec((tm, y_hg.shape[1]), lambda i: (i, 0)),
            pl.BlockSpec((tm, o_mla.shape[1]), lambda i: (i, 0)),
            pl.BlockSpec((tm, d), lambda i: (i, 0)),
            pl.BlockSpec((1, o_mla.shape[1]), lambda i: (0, 0)),
            _layer_spec(w_out.shape[1:], lambda i: (0, 0), layer, pipeline_mode=pl.Buffered(1)),
            pl.BlockSpec((1, d), lambda i: (0, 0)),
        ],
        out_specs=pl.BlockSpec((tm, d), lambda i: (i, 0)),
        out_shape=jax.ShapeDtypeStruct((TOKENS, d), F32),
        scratch_shapes=[pltpu.VMEM(w_out.shape[1:], BF16)],
        compiler_params=_cparams("arbitrary"),
        name="out_proj",
    )(y_gla, y_hg, o_mla, x, mla_gain, w_out, post_gain)


def _ffn_kernel(x_ref, ug_ref, wg_ref, wu_ref, wd_ref, pg_ref, o_ref, u_ref):
    j = pl.program_id(1)
    last = pl.num_programs(1) - 1
    half = x_ref.shape[0] // 2

    def step(first, final):
        wg, wu, wd = wg_ref[...].astype(BF16), wu_ref[...].astype(BF16), wd_ref[...].astype(BF16)
        for rows in (slice(0, half), slice(half, 2 * half)):
            if first:
                u = _rms(x_ref[rows, :], ug_ref[...]).astype(BF16)
                u_ref[rows, :] = u
            else:
                u = u_ref[rows, :]
            gate = _dot(u, wg)
            up = _dot(u, wu)
            part = _dot((gate * _sigmoid(gate) * up).astype(BF16), wd)
            acc = part if first else o_ref[rows, :] + part
            o_ref[rows, :] = x_ref[rows, :] + _rms(acc, pg_ref[...]) if final else acc

    @pl.when(j == 0)
    def _():
        step(first=True, final=False)

    @pl.when(jnp.logical_and(j > 0, j < last))
    def _():
        step(first=False, final=False)

    @pl.when(j == last)
    def _():
        step(first=False, final=True)


def _ffn(x, pre_gain, wg, wu, wd, layer, post_gain):
    tm, tf = FFN_TM, FFN_TF
    d = D_MODEL
    once = pl.Buffered(1)
    return pl.pallas_call(
        _ffn_kernel,
        grid=(TOKENS // tm, D_FF // tf),
        in_specs=[
            pl.BlockSpec((tm, d), lambda i, j: (i, 0), pipeline_mode=once),
            pl.BlockSpec((1, d), lambda i, j: (0, 0)),
            _layer_spec((d, tf), lambda i, j: (0, j), layer),
            _layer_spec((d, tf), lambda i, j: (0, j), layer),
            _layer_spec((tf, d), lambda i, j: (j, 0), layer),
            pl.BlockSpec((1, d), lambda i, j: (0, 0)),
        ],
        out_specs=pl.BlockSpec((tm, d), lambda i, j: (i, 0), pipeline_mode=once),
        out_shape=jax.ShapeDtypeStruct((TOKENS, d), F32),
        scratch_shapes=[pltpu.VMEM((tm, d), BF16)],
        compiler_params=_cparams("parallel", "arbitrary"),
        name="ffn",
    )(x, pre_gain, wg, wu, wd, post_gain)


def _rot_cols(w):
    half = MLA_ROPE // 2
    return jnp.concatenate([-w[..., half:], w[..., :half]], axis=-1)


W_IN_SIZES = (GLA_HEADS * GLA_DK, GLA_HEADS * GLA_DK, GLA_HEADS * GLA_DV, GLA_GATE_RANK,
              GLA_HEADS * GLA_DV, HGRN_HEADS * HGRN_DK, HGRN_HEADS * HGRN_DK, HGRN_HEADS * HGRN_DV,
              HGRN_HEADS * HGRN_DV, MLA_Q_RANK, MLA_KV_RANK, MLA_ROPE)
W_IN_COLS = 256


def _layout_w_in_kernel(wt_ref, o_ref):
    wt = wt_ref[...]
    cols = wt.shape[1]
    off = [int(c) for c in np.cumsum((0,) + W_IN_SIZES)]
    gq, g_low, g_out, kpe = off[0], off[3], off[4], off[11]
    half = MLA_ROPE // 2
    pe = wt[kpe:kpe + MLA_ROPE]
    rows = [wt[gq:g_low],
            wt[g_out:kpe],
            pe, -pe[half:], pe[:half],
            wt[g_low:g_out], jnp.zeros((LANE - GLA_GATE_RANK, cols), F32),
            jnp.zeros((PROJ_COLS - (CB_GLOW + 1) * LANE, cols), F32)]
    w = jnp.concatenate(rows, axis=0).T.astype(o_ref.dtype)
    for t in range(PROJ_COLS // PROJ_TN):
        o_ref[t] = w[:, t * PROJ_TN:(t + 1) * PROJ_TN]


def _layout_w_in(w_in):
    depth, d, n = w_in.shape
    tiles = PROJ_COLS // PROJ_TN
    w_in_t = jnp.swapaxes(w_in, 1, 2)
    return pl.pallas_call(
        _layout_w_in_kernel,
        grid=(depth, d // W_IN_COLS),
        in_specs=[pl.BlockSpec((None, n, W_IN_COLS), lambda l, i: (l, 0, i))],
        out_specs=pl.BlockSpec((None, tiles, W_IN_COLS, PROJ_TN), lambda l, i: (l, 0, i, 0)),
        out_shape=jax.ShapeDtypeStruct((depth, tiles, d, PROJ_TN), BF16),
        compiler_params=_cparams("parallel", "parallel"),
        name="layout_w_in",
    )(w_in_t)


def _layout_wq(wq_b):
    lead = wq_b.shape[:-1]
    w = wq_b.reshape(lead + (MLA_HEADS, MLA_NOPE + MLA_ROPE))
    nope, pe = w[..., :MLA_NOPE], w[..., MLA_NOPE:]
    w = jnp.concatenate([nope, pe, _rot_cols(pe)], axis=-1)
    return w.reshape(lead + (MLA_HEADS * 2 * LANE,)).astype(BF16)


def _layout_wkv(wkv_b):
    lead = wkv_b.shape[:-1]
    w = wkv_b.reshape(lead + (MLA_HEADS, MLA_NOPE + MLA_DV))
    k = w[..., :MLA_NOPE].reshape(lead + (MLA_HEADS * MLA_NOPE,))
    v = w[..., MLA_NOPE:].reshape(lead + (MLA_HEADS * MLA_DV,))
    return k.astype(BF16), jnp.swapaxes(v, -1, -2).astype(BF16)


def kernel(x, positions, attn_pre_norm, w_in, gla_gate_w2, gla_gate_b, gla_out_norm, hgrn_lb_logits,
           hgrn_out_norm, mla_q_norm, mla_wq_b, mla_kv_norm, mla_wkv_b, mla_out_norm, w_out,
           attn_post_norm, ffn_pre_norm, w_gate, w_up, w_down, ffn_post_norm):
    depth = w_in.shape[0]
    inv_freq = ROPE_THETA ** (-jnp.arange(0, MLA_ROPE, 2, dtype=F32) / MLA_ROPE)
    ang = positions.astype(F32).reshape(TOKENS, 1) * inv_freq
    cs = jnp.concatenate([jnp.cos(ang), jnp.cos(ang), jnp.sin(ang), jnp.sin(ang)], axis=-1)
    csum = jnp.cumsum(jax.nn.softmax(hgrn_lb_logits.astype(F32), axis=0), axis=0)
    lb = csum - csum[0]
    log_lb = jnp.log(lb)
    log_1m_lb = jnp.log1p(-lb)

    w_in_l = _layout_w_in(w_in)
    wq_l = _layout_wq(mla_wq_b)
    wk_l, wvt_l = _layout_wkv(mla_wkv_b)
    w2 = jnp.pad(gla_gate_w2, [(0, 0), (0, LANE - GLA_GATE_RANK), (0, 0)])
    b2 = gla_gate_b
    sums = jnp.asarray(_decay_sum_matrix(), dtype=BF16)

    def row(p, l):
        return p[l].reshape(1, -1)

    xf = x.reshape(TOKENS, D_MODEL)
    for l in range(depth):
        proj = _norm_matmul(xf, row(attn_pre_norm, l), w_in_l, l, F32)
        y_gla = _rec_call(
            "gla", proj,
            [(CB_GQ, 2), (CB_GK, 2), (CB_GV, 4), (CB_GLOW, 1), (CB_GOUT, 4)],
            [w2[l], row(b2, l), row(gla_out_norm, l), sums])
        y_hg = _rec_call(
            "hgrn", proj,
            [(CB_HQ, 4), (CB_HF, 4), (CB_HI, 4), (CB_HOUT, 4)],
            [row(log_lb, l), row(log_1m_lb, l), row(hgrn_out_norm, l), sums])
        q, kn, vt, kr = _mla_proj(proj, cs, row(mla_q_norm, l), row(mla_kv_norm, l), wq_l, wk_l, wvt_l, l)
        o_mla = _attention(q, kn, kr, vt)
        xf = _out_proj(y_gla, y_hg, o_mla, xf, row(mla_out_norm, l), w_out, l, row(attn_post_norm, l))
        xf = _ffn(xf, row(ffn_pre_norm, l), w_gate, w_up, w_down, l, row(ffn_post_norm, l))
    return xf.reshape(x.shape)
```

```python
import functools

import numpy as np
import jax
import jax.numpy as jnp
from jax import lax
from jax.experimental import pallas as pl
from jax.experimental.pallas import tpu as pltpu

F32 = jnp.float32
BF16 = jnp.bfloat16

D_MODEL = 2048
BATCH = 2
SEQ = 4096
DEPTH = 4
TOKENS = BATCH * SEQ
GLA_HEADS = 4
GLA_DK = 64
GLA_DV = 128
GLA_GATE_RANK = 16
GLA_GATE_NORMALIZER = 16.0
HGRN_HEADS = 4
HGRN_DK = 128
HGRN_DV = 128
MLA_HEADS = 8
MLA_Q_RANK = 512
MLA_KV_RANK = 512
MLA_NOPE = 128
MLA_ROPE = 64
MLA_DV = 128
ROPE_THETA = 10000.0
D_FF = -(-8 * D_MODEL // (3 * 256)) * 256
EPS = 1e-6
LOG2_E = float(np.log2(np.e))

LANE = 128
VMEM_LIMIT = 60 * 1024 * 1024

CB_GQ, CB_GK, CB_GV, CB_GOUT = 0, 2, 4, 8
CB_HQ, CB_HF, CB_HI, CB_HOUT = 12, 16, 20, 24
CB_QC, CB_KVC = 28, 32
CB_KPE, CB_GLOW = 36, 37
PROJ_COLS = 40 * LANE

PROJ_TM, PROJ_TN = 1024, 1280
REC_HEADS = 4
REC_CHUNK = 128
REC_TB = 1024
REC_GROUP = REC_TB // REC_CHUNK
REC_LEVELS = tuple(2 ** p for p in range(1, 8))
MLA_TM = 512
ATT_TQ = 512
ATT_HEADS = 4
VT_ROWS = MLA_DV + 16
OUT_TM = 512
FFN_TM, FFN_TF = 1024, 512


def _cparams(*sem):
    return pltpu.CompilerParams(dimension_semantics=sem, vmem_limit_bytes=VMEM_LIMIT)


def _dot(a, b):
    return jnp.dot(a, b, preferred_element_type=F32)


def _dot_nt(a, b):
    return lax.dot_general(a, b, (((1,), (1,)), ((), ())), preferred_element_type=F32)


def _dot_tn(a, b):
    return lax.dot_general(a, b, (((0,), (0,)), ((), ())), preferred_element_type=F32)


def _split_bf16(x):
    hi = x.astype(BF16)
    lo = (x - hi.astype(F32)).astype(BF16)
    return hi, lo


def _rms(x, gain):
    ms = jnp.mean(x * x, axis=-1, keepdims=True)
    return x * lax.rsqrt(ms + EPS) * gain


def _sigmoid(x):
    return 1.0 / (1.0 + jnp.exp(-x))


def _softplus_neg_abs(x):
    return jnp.log(1.0 + jnp.exp(-jnp.abs(x)))


def _log_sigmoid(x):
    return jnp.minimum(x, 0.0) - _softplus_neg_abs(x)


def _layer_spec(block, index_map, layer, **kwargs):
    return pl.BlockSpec((None,) + block, lambda *g: (layer,) + index_map(*g), **kwargs)


def _norm_matmul_kernel(x_ref, g_ref, w_ref, o_ref, h_ref):
    j = pl.program_id(1)

    @pl.when(j == 0)
    def _():
        half = x_ref.shape[0] // 2
        for rows in (slice(0, half), slice(half, 2 * half)):
            h = _rms(x_ref[rows, :], g_ref[...]).astype(BF16)
            h_ref[rows, :] = h
            o_ref[rows, :] = _dot(h, w_ref[0]).astype(o_ref.dtype)

    @pl.when(j > 0)
    def _():
        o_ref[...] = _dot(h_ref[...], w_ref[j]).astype(o_ref.dtype)


def _norm_matmul(x, gain, w, layer, out_dtype):
    m, d = x.shape
    tiles = w.shape[1]
    n = tiles * PROJ_TN
    return pl.pallas_call(
        _norm_matmul_kernel,
        grid=(m // PROJ_TM, tiles),
        in_specs=[
            pl.BlockSpec((PROJ_TM, d), lambda i, j: (i, 0)),
            pl.BlockSpec((1, d), lambda i, j: (0, 0)),
            _layer_spec((tiles, d, PROJ_TN), lambda i, j: (0, 0, 0), layer, pipeline_mode=pl.Buffered(1)),
        ],
        out_specs=pl.BlockSpec((PROJ_TM, PROJ_TN), lambda i, j: (i, j)),
        out_shape=jax.ShapeDtypeStruct((m, n), out_dtype),
        scratch_shapes=[pltpu.VMEM((PROJ_TM, d), BF16)],
        compiler_params=_cparams("parallel", "arbitrary"),
        name="norm_proj",
    )(x, gain, w)


def _decay_sum_matrix():
    c = REC_CHUNK
    t = np.arange(c)
    blocks = [(t[None, :] <= t[:, None])]
    for m in REC_LEVELS:
        half = m // 2
        pos = t % m
        mid = t - pos + half
        upper = (pos >= half)[:, None] & (t[None, :] >= mid[:, None]) & (t[None, :] <= t[:, None])
        lower = (pos < half)[:, None] & (t[None, :] > t[:, None]) & (t[None, :] < mid[:, None])
        blocks.append(upper | lower)
    blocks.append(t[None, :] > t[:, None])
    sums = np.concatenate(blocks, axis=0).astype(np.float32)
    return np.concatenate([sums, sums], axis=1)


def _pair_level_codes():
    c = REC_CHUNK
    ri = lax.broadcasted_iota(jnp.int32, (c, c), 0)
    ci = lax.broadcasted_iota(jnp.int32, (c, c), 1)
    lvl = 32 - lax.clz(ri ^ ci)
    code = jnp.where(ri > ci, lvl, jnp.where(ri == ci, 0, -1))
    return code.astype(F32).astype(BF16)


def _rec_finish(o, og, gain):
    return _rms(o, gain) * (og * _sigmoid(og))


def _rec_kernel(kind, *refs):
    if kind == "gla":
        q_ref, k_ref, v_ref, gl_ref, og_ref, w2_ref, b2_ref, gain_ref, sums_ref, y_ref, st_ref = refs
    else:
        q_ref, f_ref, v_ref, og_ref, llb_ref, l1m_ref, gain_ref, sums_ref, y_ref, st_ref = refs

    @pl.when(pl.program_id(1) == 0)
    def _():
        st_ref[...] = jnp.zeros_like(st_ref)

    c = REC_CHUNK
    nl = len(REC_LEVELS)
    sums = sums_ref[...]
    gain = gain_ref[...]
    code = _pair_level_codes()
    chunks = [slice(ck * c, (ck + 1) * c) for ck in range(REC_TB // c)]
    lanes = [slice(h * LANE, (h + 1) * LANE) for h in range(REC_HEADS)]
    pack = LANE // (GLA_DK if kind == "gla" else HGRN_DK)
    qk_lanes = [slice((h // pack) * LANE, (h // pack + 1) * LANE) for h in range(REC_HEADS)]
    if pack > 1:
        lane = lax.broadcasted_iota(jnp.int32, (1, LANE), 1) // (LANE // pack)
        kmask = [jnp.where(lane == h % pack, 1.0, 0.0).astype(BF16) for h in range(REC_HEADS)]
    else:
        kmask = [None] * REC_HEADS

    def head_k(k_all, h):
        blk = k_all[:, qk_lanes[h]]
        return blk if kmask[h] is None else blk * kmask[h]

    if kind == "gla":
        w2_hi, w2_lo = _split_bf16(w2_ref[...])

    st = [st_ref[h] for h in range(REC_HEADS)]
    for g0 in range(0, len(chunks), REC_GROUP):
        group = chunks[g0:g0 + REC_GROUP]
        pairs = [(ck, h) for ck in range(len(group)) for h in range(REC_HEADS)]
        qs, ks, es = [], [], []
        for sl in group:
            if kind == "gla":
                gl_hi, gl_lo = _split_bf16(gl_ref[sl, :])
                z = _dot(gl_hi, w2_hi) + _dot(gl_lo, w2_hi) + _dot(gl_hi, w2_lo) + b2_ref[...]
                g = _log_sigmoid(z) * (1.0 / GLA_GATE_NORMALIZER)
                q = q_ref[sl, :] * (GLA_DK ** -0.5)
                k = k_ref[sl, :]
            else:
                hq = q_ref[sl, :]
                q = hq * _sigmoid(hq)
                lo = llb_ref[...]
                hi = l1m_ref[...] + _log_sigmoid(f_ref[sl, :])
                g = jnp.maximum(lo, hi) + _softplus_neg_abs(lo - hi)
                k = 1.0 - jnp.exp(g)
            g_hi, g_lo = _split_bf16(g * LOG2_E)
            e_log2 = _dot(sums, jnp.concatenate([g_hi, g_lo], axis=0))
            es.append((jnp.exp2(e_log2[0:c]), jnp.exp2(e_log2[c:(nl + 1) * c].astype(BF16)),
                       jnp.exp2(e_log2[(nl + 1) * c:(nl + 2) * c])))
            qs.append(q)
            ks.append(k)

        q16 = [q.astype(BF16) for q in qs]
        k16 = {(ck, h): head_k(ks[ck].astype(BF16), h) for ck, h in pairs}
        raw = {(ck, h): _dot_nt(q16[ck][:, qk_lanes[h]], k16[ck, h]) for ck, h in pairs}
        a16 = {p: jnp.where(code == 0, raw[p].astype(BF16), jnp.zeros((c, c), BF16)) for p in pairs}
        for lvl in range(1, nl + 1):
            e16 = [e[1][(lvl - 1) * c:lvl * c] for e in es]
            qm = [q16[ck] * e16[ck] for ck in range(len(group))]
            raw = {(ck, h): _dot_nt(qm[ck][:, qk_lanes[h]], k16[ck, h] * e16[ck][:, qk_lanes[h]]) for ck, h in pairs}
            a16 = {p: jnp.where(code == lvl, raw[p].astype(BF16), a16[p]) for p in pairs}

        for ck, sl in enumerate(group):
            e_cum, _, e_tail = es[ck]
            q_in = (qs[ck] * e_cum).astype(BF16)
            k_out = (ks[ck] * e_tail).astype(BF16)
            v16 = v_ref[sl, :].astype(BF16)
            for h in range(REC_HEADS):
                ls, kl = lanes[h], qk_lanes[h]
                o = _dot(a16[ck, h], v16[:, ls]) + _dot_nt(q_in[:, kl], st[h].astype(BF16))
                st[h] = st[h] * e_cum[c - 1:c, kl] + _dot_tn(v16[:, ls], head_k(k_out, h))
                y_ref[sl, ls] = _rec_finish(o, og_ref[sl, ls], gain).astype(y_ref.dtype)
    for h in range(REC_HEADS):
        st_ref[h] = st[h]


def _rec_call(kind, proj, col_blocks, params):
    nt = SEQ // REC_TB
    in_specs, args = [], []
    for cb, width in col_blocks:
        in_specs.append(pl.BlockSpec((REC_TB, width * LANE),
                                     lambda b, t, cb=cb, width=width: (b * nt + t, cb // width)))
        args.append(proj)
    for p in params:
        in_specs.append(pl.BlockSpec(p.shape, lambda b, t: (0, 0)))
        args.append(p)
    wide = REC_HEADS * LANE
    return pl.pallas_call(
        functools.partial(_rec_kernel, kind),
        grid=(BATCH, nt),
        in_specs=in_specs,
        out_specs=pl.BlockSpec((REC_TB, wide), lambda b, t: (b * nt + t, 0)),
        out_shape=jax.ShapeDtypeStruct((TOKENS, wide), BF16),
        scratch_shapes=[pltpu.VMEM((REC_HEADS, LANE, LANE), F32)],
        compiler_params=_cparams("parallel", "arbitrary"),
        name=kind + "_recurrence",
    )(*args)


def _mla_proj_kernel(qc_ref, kvc_ref, kpe_ref, cs_ref, qn_ref, kvn_ref, wq_ref, wk_ref, wvt_ref,
                     q_ref, kn_ref, vt_ref, kr_ref):
    scale = (MLA_NOPE + MLA_ROPE) ** -0.5
    cs = cs_ref[...]
    qn = _rms(qc_ref[...], qn_ref[...]).astype(BF16)
    q = _dot(qn, wq_ref[...])
    for h in range(MLA_HEADS):
        lo = h * 2 * LANE
        q_ref[:, lo:lo + LANE] = (q[:, lo:lo + LANE] * scale).astype(q_ref.dtype)
        q_ref[:, lo + LANE:lo + 2 * LANE] = (q[:, lo + LANE:lo + 2 * LANE] * (cs * scale)).astype(q_ref.dtype)
    kvn = _rms(kvc_ref[...], kvn_ref[...]).astype(BF16)
    kn_ref[...] = _dot(kvn, wk_ref[...]).astype(kn_ref.dtype)
    vt = _dot_nt(wvt_ref[...], kvn)
    ones = jnp.ones((VT_ROWS - MLA_DV, vt.shape[1]), vt_ref.dtype)
    for h in range(MLA_HEADS):
        vt_ref[h * VT_ROWS:h * VT_ROWS + MLA_DV, :] = vt[h * MLA_DV:(h + 1) * MLA_DV].astype(vt_ref.dtype)
        vt_ref[h * VT_ROWS + MLA_DV:(h + 1) * VT_ROWS, :] = ones
    t = kpe_ref[...] * cs
    kr_ref[...] = (t + pltpu.roll(t, LANE // 2, 1)).astype(kr_ref.dtype)


def _mla_proj(proj, cs, qn_gain, kvn_gain, wq, wk, wvt, layer):
    tm = MLA_TM
    wide = 4 * LANE
    nq = MLA_HEADS * 2 * LANE
    nk = MLA_HEADS * MLA_NOPE
    nvt = MLA_HEADS * VT_ROWS
    return pl.pallas_call(
        _mla_proj_kernel,
        grid=(TOKENS // tm,),
        in_specs=[
            pl.BlockSpec((tm, wide), lambda i: (i, CB_QC // 4)),
            pl.BlockSpec((tm, wide), lambda i: (i, CB_KVC // 4)),
            pl.BlockSpec((tm, LANE), lambda i: (i, CB_KPE)),
            pl.BlockSpec((tm, LANE), lambda i: (i, 0)),
            pl.BlockSpec((1, MLA_Q_RANK), lambda i: (0, 0)),
            pl.BlockSpec((1, MLA_KV_RANK), lambda i: (0, 0)),
            _layer_spec(wq.shape[1:], lambda i: (0, 0), layer),
            _layer_spec(wk.shape[1:], lambda i: (0, 0), layer),
            _layer_spec(wvt.shape[1:], lambda i: (0, 0), layer),
        ],
        out_specs=[
            pl.BlockSpec((tm, nq), lambda i: (i, 0)),
            pl.BlockSpec((tm, nk), lambda i: (i, 0)),
            pl.BlockSpec((nvt, tm), lambda i: (0, i)),
            pl.BlockSpec((tm, LANE), lambda i: (i, 0)),
        ],
        out_shape=[
            jax.ShapeDtypeStruct((TOKENS, nq), BF16),
            jax.ShapeDtypeStruct((TOKENS, nk), BF16),
            jax.ShapeDtypeStruct((nvt, TOKENS), BF16),
            jax.ShapeDtypeStruct((TOKENS, LANE), BF16),
        ],
        compiler_params=_cparams("parallel"),
        name="mla_proj",
    )(proj, proj, proj, cs, qn_gain, kvn_gain, wq, wk, wvt)


def _attn_kernel(q_ref, kn_ref, kr_ref, vt_ref, o_ref, s0_ref, s1_ref, m_ref, acc_ref):
    tq = ATT_TQ
    qi = pl.program_id(2)
    qs = [q_ref[:, h * 2 * LANE:(h + 1) * 2 * LANE] for h in range(ATT_HEADS)]

    def scores(j, s_ref):
        start = pl.multiple_of(j * tq, tq)
        kr = kr_ref[pl.ds(start, tq), :]
        for h in range(ATT_HEADS):
            k = jnp.concatenate([kn_ref[pl.ds(start, tq), h * LANE:(h + 1) * LANE], kr], axis=1)
            s_ref[h] = _dot_nt(k, qs[h])

    def absorb(j, s_ref, masked):
        start = pl.multiple_of(j * tq, tq)
        for h in range(ATT_HEADS):
            s = s_ref[h]
            if masked:
                ki = lax.broadcasted_iota(jnp.int32, s.shape, 0)
                qj = lax.broadcasted_iota(jnp.int32, s.shape, 1)
                s = jnp.where(ki <= qj, s, -jnp.inf)
            m = m_ref[h]
            m_new = jnp.maximum(m, jnp.max(s, axis=0, keepdims=True))
            alpha = jnp.exp(m - m_new)
            p = jnp.exp((s - m_new).astype(BF16))
            m_ref[h] = m_new
            acc_ref[h] = alpha * acc_ref[h] + _dot(vt_ref[h * VT_ROWS:(h + 1) * VT_ROWS, pl.ds(start, tq)], p)

    m_ref[...] = jnp.full(m_ref.shape, -jnp.inf, F32)
    acc_ref[...] = jnp.zeros(acc_ref.shape, F32)
    scores(0, s0_ref)

    def pair(t, carry):
        j = 2 * t
        scores(j + 1, s1_ref)
        absorb(j, s0_ref, masked=False)
        scores(j + 2, s0_ref)
        absorb(j + 1, s1_ref, masked=False)
        return carry

    lax.fori_loop(0, qi // 2, pair, 0)

    @pl.when(qi % 2 == 1)
    def _():
        scores(qi, s1_ref)
        absorb(qi - 1, s0_ref, masked=False)
        absorb(qi, s1_ref, masked=True)

    @pl.when(qi % 2 == 0)
    def _():
        absorb(qi, s0_ref, masked=True)

    for h in range(ATT_HEADS):
        acc = acc_ref[h]
        o = acc[:MLA_DV] / acc[MLA_DV:MLA_DV + 1]
        o_ref[:, h * LANE:(h + 1) * LANE] = o.T.astype(o_ref.dtype)


def _attention(q, kn, kr, vt):
    tq = ATT_TQ
    nq = SEQ // tq
    hw = ATT_HEADS * LANE
    return pl.pallas_call(
        _attn_kernel,
        grid=(BATCH, MLA_HEADS // ATT_HEADS, nq),
        in_specs=[
            pl.BlockSpec((tq, 2 * hw), lambda b, h, i: (b * nq + i, h)),
            pl.BlockSpec((SEQ, hw), lambda b, h, i: (b, h)),
            pl.BlockSpec((SEQ, LANE), lambda b, h, i: (b, 0)),
            pl.BlockSpec((ATT_HEADS * VT_ROWS, SEQ), lambda b, h, i: (h, b)),
        ],
        out_specs=pl.BlockSpec((tq, hw), lambda b, h, i: (b * nq + i, h)),
        out_shape=jax.ShapeDtypeStruct((TOKENS, MLA_HEADS * MLA_DV), F32),
        scratch_shapes=[pltpu.VMEM((ATT_HEADS, tq, tq), F32), pltpu.VMEM((ATT_HEADS, tq, tq), F32),
                        pltpu.VMEM((ATT_HEADS, 1, tq), F32), pltpu.VMEM((ATT_HEADS, VT_ROWS, tq), F32)],
        compiler_params=_cparams("parallel", "parallel", "arbitrary"),
        name="mla_attention",
    )(q, kn, kr, vt)


def _out_proj_kernel(yg_ref, yh_ref, om_ref, x_ref, mg_ref, w_ref, pg_ref, o_ref, w16_ref):
    @pl.when(pl.program_id(0) == 0)
    def _():
        w16_ref[...] = w_ref[...].astype(BF16)

    ym = _rms(om_ref[...], mg_ref[...]).astype(BF16)
    mix = jnp.concatenate([yg_ref[...], yh_ref[...], ym], axis=1)
    m = _dot(mix, w16_ref[...])
    o_ref[...] = x_ref[...] + _rms(m, pg_ref[...])


def _out_proj(y_gla, y_hg, o_mla, x, mla_gain, w_out, layer, post_gain):
    tm = OUT_TM
    d = D_MODEL
    return pl.pallas_call(
        _out_proj_kernel,
        grid=(TOKENS // tm,),
        in_specs=[
            pl.BlockSpec((tm, y_gla.shape[1]), lambda i: (i, 0)),
            pl.BlockSpec((tm, y_hg.shape[1]), lambda i: (i, 0)),
            pl.BlockSpec((tm, o_mla.shape[1]), lambda i: (i, 0)),
            pl.BlockSpec((tm, d), lambda i: (i, 0)),
            pl.BlockSpec((1, o_mla.shape[1]), lambda i: (0, 0)),
            _layer_spec(w_out.shape[1:], lambda i: (0, 0), layer, pipeline_mode=pl.Buffered(1)),
            pl.BlockSpec((1, d), lambda i: (0, 0)),
        ],
        out_specs=pl.BlockSpec((tm, d), lambda i: (i, 0)),
        out_shape=jax.ShapeDtypeStruct((TOKENS, d), F32),
        scratch_shapes=[pltpu.VMEM(w_out.shape[1:], BF16)],
        compiler_params=_cparams("arbitrary"),
        name="out_proj",
    )(y_gla, y_hg, o_mla, x, mla_gain, w_out, post_gain)


def _ffn_kernel(x_ref, ug_ref, wg_ref, wu_ref, wd_ref, pg_ref, o_ref, u_ref):
    j = pl.program_id(1)
    last = pl.num_programs(1) - 1
    half = x_ref.shape[0] // 2

    def step(first, final):
        wg, wu, wd = wg_ref[...].astype(BF16), wu_ref[...].astype(BF16), wd_ref[...].astype(BF16)
        for rows in (slice(0, half), slice(half, 2 * half)):
            if first:
                u = _rms(x_ref[rows, :], ug_ref[...]).astype(BF16)
                u_ref[rows, :] = u
            else:
                u = u_ref[rows, :]
            gate = _dot(u, wg)
            up = _dot(u, wu)
            part = _dot((gate * _sigmoid(gate) * up).astype(BF16), wd)
            acc = part if first else o_ref[rows, :] + part
            o_ref[rows, :] = x_ref[rows, :] + _rms(acc, pg_ref[...]) if final else acc

    @pl.when(j == 0)
    def _():
        step(first=True, final=False)

    @pl.when(jnp.logical_and(j > 0, j < last))
    def _():
        step(first=False, final=False)

    @pl.when(j == last)
    def _():
        step(first=False, final=True)


def _ffn(x, pre_gain, wg, wu, wd, layer, post_gain):
    tm, tf = FFN_TM, FFN_TF
    d = D_MODEL
    once = pl.Buffered(1)
    return pl.pallas_call(
        _ffn_kernel,
        grid=(TOKENS // tm, D_FF // tf),
        in_specs=[
            pl.BlockSpec((tm, d), lambda i, j: (i, 0), pipeline_mode=once),
            pl.BlockSpec((1, d), lambda i, j: (0, 0)),
            _layer_spec((d, tf), lambda i, j: (0, j), layer),
            _layer_spec((d, tf), lambda i, j: (0, j), layer),
            _layer_spec((tf, d), lambda i, j: (j, 0), layer),
            pl.BlockSpec((1, d), lambda i, j: (0, 0)),
        ],
        out_specs=pl.BlockSpec((tm, d), lambda i, j: (i, 0), pipeline_mode=once),
        out_shape=jax.ShapeDtypeStruct((TOKENS, d), F32),
        scratch_shapes=[pltpu.VMEM((tm, d), BF16)],
        compiler_params=_cparams("parallel", "arbitrary"),
        name="ffn",
    )(x, pre_gain, wg, wu, wd, post_gain)


def _rot_cols(w):
    half = MLA_ROPE // 2
    return jnp.concatenate([-w[..., half:], w[..., :half]], axis=-1)


W_IN_SIZES = (GLA_HEADS * GLA_DK, GLA_HEADS * GLA_DK, GLA_HEADS * GLA_DV, GLA_GATE_RANK,
              GLA_HEADS * GLA_DV, HGRN_HEADS * HGRN_DK, HGRN_HEADS * HGRN_DK, HGRN_HEADS * HGRN_DV,
              HGRN_HEADS * HGRN_DV, MLA_Q_RANK, MLA_KV_RANK, MLA_ROPE)
W_IN_COLS = 256


def _layout_w_in_kernel(wt_ref, o_ref):
    wt = wt_ref[...]
    cols = wt.shape[1]
    off = [int(c) for c in np.cumsum((0,) + W_IN_SIZES)]
    gq, g_low, g_out, kpe = off[0], off[3], off[4], off[11]
    half = MLA_ROPE // 2
    pe = wt[kpe:kpe + MLA_ROPE]
    rows = [wt[gq:g_low],
            wt[g_out:kpe],
            pe, -pe[half:], pe[:half],
            wt[g_low:g_out], jnp.zeros((LANE - GLA_GATE_RANK, cols), F32),
            jnp.zeros((PROJ_COLS - (CB_GLOW + 1) * LANE, cols), F32)]
    w = jnp.concatenate(rows, axis=0).T.astype(o_ref.dtype)
    for t in range(PROJ_COLS // PROJ_TN):
        o_ref[t] = w[:, t * PROJ_TN:(t + 1) * PROJ_TN]


def _layout_w_in(w_in):
    depth, d, n = w_in.shape
    tiles = PROJ_COLS // PROJ_TN
    w_in_t = jnp.swapaxes(w_in, 1, 2)
    return pl.pallas_call(
        _layout_w_in_kernel,
        grid=(depth, d // W_IN_COLS),
        in_specs=[pl.BlockSpec((None, n, W_IN_COLS), lambda l, i: (l, 0, i))],
        out_specs=pl.BlockSpec((None, tiles, W_IN_COLS, PROJ_TN), lambda l, i: (l, 0, i, 0)),
        out_shape=jax.ShapeDtypeStruct((depth, tiles, d, PROJ_TN), BF16),
        compiler_params=_cparams("parallel", "parallel"),
        name="layout_w_in",
    )(w_in_t)


def _layout_wq(wq_b):
    lead = wq_b.shape[:-1]
    w = wq_b.reshape(lead + (MLA_HEADS, MLA_NOPE + MLA_ROPE))
    nope, pe = w[..., :MLA_NOPE], w[..., MLA_NOPE:]
    w = jnp.concatenate([nope, pe, _rot_cols(pe)], axis=-1)
    return w.reshape(lead + (MLA_HEADS * 2 * LANE,)).astype(BF16)


def _layout_wkv(wkv_b):
    lead = wkv_b.shape[:-1]
    w = wkv_b.reshape(lead + (MLA_HEADS, MLA_NOPE + MLA_DV))
    k = w[..., :MLA_NOPE].reshape(lead + (MLA_HEADS * MLA_NOPE,))
    v = w[..., MLA_NOPE:].reshape(lead + (MLA_HEADS * MLA_DV,))
    return k.astype(BF16), jnp.swapaxes(v, -1, -2).astype(BF16)


def kernel(x, positions, attn_pre_norm, w_in, gla_gate_w2, gla_gate_b, gla_out_norm, hgrn_lb_logits,
           hgrn_out_norm, mla_q_norm, mla_wq_b, mla_kv_norm, mla_wkv_b, mla_out_norm, w_out,
           attn_post_norm, ffn_pre_norm, w_gate, w_up, w_down, ffn_post_norm):
    depth = w_in.shape[0]
    inv_freq = ROPE_THETA ** (-jnp.arange(0, MLA_ROPE, 2, dtype=F32) / MLA_ROPE)
    ang = positions.astype(F32).reshape(TOKENS, 1) * inv_freq
    cs = jnp.concatenate([jnp.cos(ang), jnp.cos(ang), jnp.sin(ang), jnp.sin(ang)], axis=-1)
    csum = jnp.cumsum(jax.nn.softmax(hgrn_lb_logits.astype(F32), axis=0), axis=0)
    lb = csum - csum[0]
    log_lb = jnp.log(lb)
    log_1m_lb = jnp.log1p(-lb)

    w_in_l = _layout_w_in(w_in)
    wq_l = _layout_wq(mla_wq_b)
    wk_l, wvt_l = _layout_wkv(mla_wkv_b)
    w2 = jnp.pad(gla_gate_w2, [(0, 0), (0, LANE - GLA_GATE_RANK), (0, 0)])
    b2 = gla_gate_b
    sums = jnp.asarray(_decay_sum_matrix(), dtype=BF16)

    def row(p, l):
        return p[l].reshape(1, -1)

    xf = x.reshape(TOKENS, D_MODEL)
    for l in range(depth):
        proj = _norm_matmul(xf, row(attn_pre_norm, l), w_in_l, l, F32)
        y_gla = _rec_call(
            "gla", proj,
            [(CB_GQ, 2), (CB_GK, 2), (CB_GV, 4), (CB_GLOW, 1), (CB_GOUT, 4)],
            [w2[l], row(b2, l), row(gla_out_norm, l), sums])
        y_hg = _rec_call(
            "hgrn", proj,
            [(CB_HQ, 4), (CB_HF, 4), (CB_HI, 4), (CB_HOUT, 4)],
            [row(log_lb, l), row(log_1m_lb, l), row(hgrn_out_norm, l), sums])
        q, kn, vt, kr = _mla_proj(proj, cs, row(mla_q_norm, l), row(mla_kv_norm, l), wq_l, wk_l, wvt_l, l)
        o_mla = _attention(q, kn, kr, vt)
        xf = _out_proj(y_gla, y_hg, o_mla, xf, row(mla_out_norm, l), w_out, l, row(attn_post_norm, l))
        xf = _ffn(xf, row(ffn_pre_norm, l), w_gate, w_up, w_down, l, row(ffn_post_norm, l))
    return xf.reshape(x.shape)
```
